```python
import math
import jax, jax.numpy as jnp
from jax import lax
import numpy as np

D_MODEL = 2048
BATCH = 8
SEQ = 4096
DEPTH = 4
DEC_BATCH = 8
DEC_SEQ = 16
PAST_LEN = 4096

CHUNK = 64
N_MIXERS = 3
N_A = (DEPTH + 2) // 3
N_B = (DEPTH + 1) // 3
N_C = DEPTH // 3
A_HEADS = 4
A_DV = D_MODEL // A_HEADS
A_DQK = A_DV // 2
A_IN = 2 * A_HEADS * A_DQK + 2 * A_HEADS * A_DV + 2 * A_HEADS
CONV_W = 3
S5_P = 16
S5_G = D_MODEL // S5_P
S5_N = 64
D_FF = -(-8 * D_MODEL // (3 * 256)) * 256
EPS = 1e-6

kernel_name = "hybrid_streaming_encoder_step"


def rmsnorm(x, g):
    xf = x.astype(jnp.float32)
    y = xf * lax.rsqrt(jnp.mean(xf * xf, axis=-1, keepdims=True) + EPS)
    return (y * g.astype(jnp.float32)).astype(x.dtype)


def to_chunks(a, lc):
    b, l = a.shape[0], a.shape[1]
    return jnp.moveaxis(a.reshape((b, l // lc, lc) + a.shape[2:]), 1, 0)


def from_chunks(a):
    a = jnp.moveaxis(a, 0, 1)
    return a.reshape((a.shape[0], a.shape[1] * a.shape[2]) + a.shape[3:])


def mlstm_chunk(carry, inp):
    C, n, m = carry
    q, k, v, li, lf = inp
    L = q.shape[1]
    b = jnp.swapaxes(jnp.cumsum(lf, axis=1), 1, 2)
    li = jnp.swapaxes(li, 1, 2)
    causal = jnp.arange(L)[:, None] >= jnp.arange(L)[None, :]
    dmat = jnp.where(causal, b[..., :, None] - b[..., None, :] + li[..., None, :], -jnp.inf)
    inter = b + m[..., None]
    m_t = jnp.maximum(inter, jnp.max(dmat, axis=-1))
    w = jnp.exp(dmat - m_t[..., None])
    s = jnp.einsum('blhd,bshd->bhls', q, k) * w
    inter_w = jnp.exp(inter - m_t)
    num = jnp.einsum('bhls,bshv->blhv', s, v) + jnp.einsum('bhl,blhd,bhdv->blhv', inter_w, q, C)
    den = jnp.sum(s, axis=-1) + inter_w * jnp.einsum('blhd,bhd->bhl', q, n)
    floor = jnp.maximum(jnp.abs(den), jnp.exp(-m_t))
    h = num / jnp.swapaxes(floor, 1, 2)[..., None]
    m_new = m_t[..., -1]
    w_last = w[..., -1, :]
    decay = inter_w[..., -1]
    C_new = decay[..., None, None] * C + jnp.einsum('bhs,bshd,bshv->bhdv', w_last, k, v)
    n_new = decay[..., None] * n + jnp.einsum('bhs,bshd->bhd', w_last, k)
    return (C_new, n_new, m_new), h


def mlstm_mixer(h, w_in, b_gates, g_hnorm, w_out, C0, n0, m0):
    f32 = jnp.float32
    bsz, L, _ = h.shape
    hk, hv = A_HEADS * A_DQK, A_HEADS * A_DV
    z = h @ w_in
    q, k, v, o, gates = jnp.split(z, [hk, 2 * hk, 2 * hk + hv, 2 * hk + 2 * hv], axis=-1)
    q = q.reshape(bsz, L, A_HEADS, A_DQK).astype(f32) * (A_DQK ** -0.5)
    k = k.reshape(bsz, L, A_HEADS, A_DQK).astype(f32)
    v = v.reshape(bsz, L, A_HEADS, A_DV).astype(f32)
    gates = gates.astype(f32) + b_gates.astype(f32)
    li = gates[..., :A_HEADS]
    lf = jax.nn.log_sigmoid(gates[..., A_HEADS:])
    lc = min(CHUNK, L)
    (C, n, m), hs = lax.scan(
        mlstm_chunk, (C0.astype(f32), n0.astype(f32), m0.astype(f32)),
        (to_chunks(q, lc), to_chunks(k, lc), to_chunks(v, lc), to_chunks(li, lc), to_chunks(lf, lc)))
    hs = rmsnorm(from_chunks(hs), g_hnorm.reshape(A_HEADS, A_DV))
    y = (hs.reshape(bsz, L, hv) * jax.nn.sigmoid(o.astype(f32))).astype(h.dtype) @ w_out
    return y, C.astype(C0.dtype), n.astype(n0.dtype), m.astype(m0.dtype)


def conv_mixer(h, w_in, w_conv, w_out, prev):
    L = h.shape[1]
    gb, gc, u = jnp.split(h @ w_in, 3, axis=-1)
    z = gc * u
    zp = jnp.concatenate([prev.astype(z.dtype), z], axis=1)
    conv = (zp[:, 0:L] * w_conv[:, 0] + zp[:, 1:L + 1] * w_conv[:, 1]
            + zp[:, 2:L + 2] * w_conv[:, 2])
    y = (gb * conv) @ w_out
    return y, zp[:, -(CONV_W - 1):].astype(prev.dtype)


def _cscan_combine(e1, e2):
    a1r, a1i, b1r, b1i = e1
    a2r, a2i, b2r, b2i = e2
    return (a1r * a2r - a1i * a2i, a1r * a2i + a1i * a2r,
            a2r * b1r - a2i * b1i + b2r, a2r * b1i + a2i * b1r + b2i)


def s5_mixer(h, a_re, a_im, b_re, b_im, c_re, c_im, d_skip, log_dt, w_out, s_re0, s_im0):
    f32 = jnp.float32
    bsz, L, _ = h.shape
    u = h.astype(f32).reshape(bsz, L, S5_G, S5_P)
    dt = jnp.exp(log_dt.astype(f32))[:, None]
    lr, lim = a_re.astype(f32), a_im.astype(f32)
    mag = jnp.exp(lr * dt)
    ab_re, ab_im = mag * jnp.cos(lim * dt), mag * jnp.sin(lim * dt)
    den = lr * lr + lim * lim
    nr = ab_re - 1.0
    fr = (nr * lr + ab_im * lim) / den
    fi = (ab_im * lr - nr * lim) / den
    br, bi = b_re.astype(f32), b_im.astype(f32)
    bb_re = fr[..., None] * br - fi[..., None] * bi
    bb_im = fr[..., None] * bi + fi[..., None] * br
    cr, ci = c_re.astype(f32), c_im.astype(f32)
    dsk = d_skip.astype(f32)

    def chunk_step(carry, uc):
        x_re, x_im = carry
        bu_re = jnp.einsum('blgp,gnp->blgn', uc, bb_re)
        bu_im = jnp.einsum('blgp,gnp->blgn', uc, bb_im)
        ar = jnp.broadcast_to(ab_re, bu_re.shape)
        ai = jnp.broadcast_to(ab_im, bu_re.shape)
        pa_re, pa_im, pb_re, pb_im = lax.associative_scan(
            _cscan_combine, (ar, ai, bu_re, bu_im), axis=1)
        st_re = pb_re + pa_re * x_re[:, None] - pa_im * x_im[:, None]
        st_im = pb_im + pa_re * x_im[:, None] + pa_im * x_re[:, None]
        y = (jnp.einsum('blgn,gpn->blgp', st_re, cr) - jnp.einsum('blgn,gpn->blgp', st_im, ci)
             + dsk * uc)
        return (st_re[:, -1], st_im[:, -1]), y

    lc = min(CHUNK, L)
    (s_re, s_im), ys = lax.scan(chunk_step, (s_re0.astype(f32), s_im0.astype(f32)), to_chunks(u, lc))
    y = jax.nn.gelu(from_chunks(ys).reshape(bsz, L, D_MODEL)).astype(h.dtype)
    za, zb = jnp.split(y @ w_out, 2, axis=-1)
    return za * jax.nn.sigmoid(zb), s_re.astype(s_re0.dtype), s_im.astype(s_im0.dtype)


def swiglu(h, wg, wu, wd):
    return (jax.nn.silu(h @ wg) * (h @ wu)) @ wd


def _trunk(x, c, st_C, st_n, st_m, st_conv, st_re, st_im, p):
    bsz = x.shape[0]
    sc = jax.nn.silu(c)
    new_C, new_n, new_m, new_conv, new_re, new_im = [], [], [], [], [], []
    for i in range(DEPTH):
        mod = (sc @ p['w_mod'][i] + p['b_mod'][i]).reshape(bsz, 6, 1, D_MODEL)
        sh1, sc1, g1, sh2, sc2, g2 = (mod[:, j] for j in range(6))
        gn = p['g_norm'][i]
        h = rmsnorm(x, gn[0]) * (1 + sc1) + sh1
        kind, j = i % N_MIXERS, i // N_MIXERS
        if kind == 0:
            out, C, n, m = mlstm_mixer(h, p['wA_in'][j], p['bA_gates'][j], p['gA_hnorm'][j],
                                       p['wA_out'][j], st_C[j], st_n[j], st_m[j])
            new_C.append(C); new_n.append(n); new_m.append(m)
        elif kind == 1:
            out, cv = conv_mixer(h, p['wB_in'][j], p['wB_conv'][j], p['wB_out'][j], st_conv[j])
            new_conv.append(cv)
        else:
            out, sr, si = s5_mixer(h, p['s5_A_re'][j], p['s5_A_im'][j], p['s5_B_re'][j],
                                   p['s5_B_im'][j], p['s5_C_re'][j], p['s5_C_im'][j],
                                   p['s5_D'][j], p['s5_log_dt'][j], p['wC_out'][j],
                                   st_re[j], st_im[j])
            new_re.append(sr); new_im.append(si)
        x = x + g1 * rmsnorm(out, gn[1])
        h = rmsnorm(x, gn[2]) * (1 + sc2) + sh2
        x = x + g2 * rmsnorm(swiglu(h, p['w_ffn_gate'][i], p['w_ffn_up'][i], p['w_ffn_down'][i]), gn[3])
    return (x, jnp.stack(new_C), jnp.stack(new_n), jnp.stack(new_m), jnp.stack(new_conv),
            jnp.stack(new_re), jnp.stack(new_im))


def setup_inputs(seed: int = 0) -> dict:
    key = jax.random.key(seed)
    ks = jax.random.split(key, 40)
    nrm = jax.random.normal
    D, F, f32 = D_MODEL, D_FF, jnp.float32
    f_bias = jnp.linspace(3.0, 6.0, A_HEADS)
    bA = jnp.concatenate([0.1 * nrm(ks[13], (N_A, A_HEADS)),
                          f_bias + 0.1 * nrm(ks[14], (N_A, A_HEADS))], axis=-1)
    return {
        "x_prompt": nrm(ks[0], (BATCH, SEQ, D), f32),
        "x_sample": nrm(ks[1], (DEC_BATCH, DEC_SEQ, D), f32),
        "state_mlstm_C": 0.3 * nrm(ks[2], (N_A, DEC_BATCH, A_HEADS, A_DQK, A_DV), f32),
        "state_mlstm_n": 0.3 * nrm(ks[3], (N_A, DEC_BATCH, A_HEADS, A_DQK), f32),
        "state_mlstm_m": 2.0 + 0.5 * nrm(ks[4], (N_A, DEC_BATCH, A_HEADS), f32),
        "state_conv": nrm(ks[5], (N_B, DEC_BATCH, CONV_W - 1, D), f32),
        "state_s5_re": 0.1 * nrm(ks[6], (N_C, DEC_BATCH, S5_G, S5_N), f32),
        "state_s5_im": 0.1 * nrm(ks[7], (N_C, DEC_BATCH, S5_G, S5_N), f32),
        "c_prompt": nrm(ks[8], (BATCH, D), f32),
        "c_sample": nrm(ks[9], (DEC_BATCH, D), f32),
        "w_mod": 0.5 * D ** -0.5 * nrm(ks[10], (DEPTH, D, 6 * D), f32),
        "b_mod": 0.02 * nrm(ks[11], (DEPTH, 6 * D), f32),
        "g_norm": 1.0 + 0.02 * nrm(ks[12], (DEPTH, 4, D), f32),
        "wA_in": D ** -0.5 * nrm(ks[15], (N_A, D, A_IN), f32),
        "bA_gates": bA.astype(f32),
        "gA_hnorm": 1.0 + 0.02 * nrm(ks[16], (N_A, A_HEADS * A_DV), f32),
        "wA_out": (A_HEADS * A_DV) ** -0.5 * nrm(ks[17], (N_A, A_HEADS * A_DV, D), f32),
        "wB_in": D ** -0.5 * nrm(ks[18], (N_B, D, 3 * D), f32),
        "wB_conv": CONV_W ** -0.5 * nrm(ks[19], (N_B, D, CONV_W), f32),
        "wB_out": D ** -0.5 * nrm(ks[20], (N_B, D, D), f32),
        "s5_A_re": -0.5 + 0.01 * nrm(ks[21], (N_C, S5_G, S5_N), f32),
        "s5_A_im": jnp.pi * jnp.arange(S5_N, dtype=f32) + 0.01 * nrm(ks[22], (N_C, S5_G, S5_N), f32),
        "s5_B_re": (2 * S5_P) ** -0.5 * nrm(ks[23], (N_C, S5_G, S5_N, S5_P), f32),
        "s5_B_im": (2 * S5_P) ** -0.5 * nrm(ks[24], (N_C, S5_G, S5_N, S5_P), f32),
        "s5_C_re": (2 * S5_N) ** -0.5 * nrm(ks[25], (N_C, S5_G, S5_P, S5_N), f32),
        "s5_C_im": (2 * S5_N) ** -0.5 * nrm(ks[26], (N_C, S5_G, S5_P, S5_N), f32),
        "s5_D": nrm(ks[27], (N_C, S5_G, S5_P), f32),
        "s5_log_dt": jax.random.uniform(ks[28], (N_C, S5_G), f32, math.log(1e-3), math.log(1e-1)),
        "wC_out": D ** -0.5 * nrm(ks[29], (N_C, D, 2 * D), f32),
        "w_ffn_gate": D ** -0.5 * nrm(ks[30], (DEPTH, D, F), f32),
        "w_ffn_up": D ** -0.5 * nrm(ks[31], (DEPTH, D, F), f32),
        "w_ffn_down": F ** -0.5 * nrm(ks[32], (DEPTH, F, D), f32),
    }


def reference(x_prompt, x_sample, state_mlstm_C, state_mlstm_n, state_mlstm_m, state_conv,
              state_s5_re, state_s5_im, c_prompt, c_sample, w_mod, b_mod, g_norm, wA_in,
              bA_gates, gA_hnorm, wA_out, wB_in, wB_conv, wB_out, s5_A_re, s5_A_im, s5_B_re,
              s5_B_im, s5_C_re, s5_C_im, s5_D, s5_log_dt, wC_out, w_ffn_gate, w_ffn_up,
              w_ffn_down):
    p = dict(w_mod=w_mod, b_mod=b_mod, g_norm=g_norm, wA_in=wA_in, bA_gates=bA_gates,
             gA_hnorm=gA_hnorm, wA_out=wA_out, wB_in=wB_in, wB_conv=wB_conv, wB_out=wB_out,
             s5_A_re=s5_A_re, s5_A_im=s5_A_im, s5_B_re=s5_B_re, s5_B_im=s5_B_im,
             s5_C_re=s5_C_re, s5_C_im=s5_C_im, s5_D=s5_D, s5_log_dt=s5_log_dt, wC_out=wC_out,
             w_ffn_gate=w_ffn_gate, w_ffn_up=w_ffn_up, w_ffn_down=w_ffn_down)
    bp, dt = x_prompt.shape[0], state_mlstm_C.dtype
    z_C = jnp.zeros((N_A, bp, A_HEADS, A_DQK, A_DV), dt)
    z_n = jnp.zeros((N_A, bp, A_HEADS, A_DQK), dt)
    z_m = jnp.zeros((N_A, bp, A_HEADS), dt)
    z_conv = jnp.zeros((N_B, bp, CONV_W - 1, D_MODEL), state_conv.dtype)
    z_re = jnp.zeros((N_C, bp, S5_G, S5_N), state_s5_re.dtype)
    z_im = jnp.zeros((N_C, bp, S5_G, S5_N), state_s5_im.dtype)
    y_prompt, pC, pn, pm, pconv, pre, pim = _trunk(x_prompt, c_prompt, z_C, z_n, z_m, z_conv,
                                                   z_re, z_im, p)
    y_sample, sC, sn, sm, sconv, sre, sim = _trunk(x_sample, c_sample, state_mlstm_C,
                                                   state_mlstm_n, state_mlstm_m, state_conv,
                                                   state_s5_re, state_s5_im, p)
    return (y_prompt, y_sample, pC, pn, pm, pconv, pre, pim, sC, sn, sm, sconv, sre, sim)
```

```python
import functools

import jax
import jax.numpy as jnp
from jax import lax
from jax.experimental import pallas as pl
from jax.experimental.pallas import tpu as pltpu

F32 = jnp.float32
BF16 = jnp.bfloat16
EPS = 1e-6
N_MIXERS = 3

LANES = 128
SUBLANES = 8
MXU_DIM_V7X = 256
VMEM_BYTES_V7X = 64 * 1024 * 1024
VMEM_CAP = VMEM_BYTES_V7X - 6 * 1024 * 1024

ROW_TILE = 512
INPROJ_COL_TILE = 512
FFN_COL_TILE = 512
MOD_COL_TILE = 1024
MLSTM_CHUNK = 256
S5_CHUNK = 32
S5_LANE_GROUP = 4


def _params(vmem_bytes, n_grid):
    limit = int(min(VMEM_CAP, max(vmem_bytes * 5 // 4 + (4 << 20), 16 << 20)))
    return pltpu.CompilerParams(dimension_semantics=("arbitrary",) * n_grid,
                                vmem_limit_bytes=limit)


def _resident(block_shape, index_map):
    return pl.BlockSpec(block_shape, index_map, pipeline_mode=pl.Buffered(1))


def _row_blocking(B, L):
    if L >= ROW_TILE:
        assert L % ROW_TILE == 0
        return 1, ROW_TILE
    assert L % SUBLANES == 0
    return B, L


def _norm_mod(x, g, sc, sh):
    ms = jnp.mean(x * x, axis=-1, keepdims=True)
    y = x * lax.rsqrt(ms + EPS) * g
    return y * (1.0 + sc) + sh


def _rms_gain(y, g):
    ms = jnp.mean(y * y, axis=-1, keepdims=True)
    return y * lax.rsqrt(ms + EPS) * g


def _mod_kernel(c_ref, w_ref, b_ref, o_ref):
    c = c_ref[...]
    sc = (c * jax.nn.sigmoid(c)).astype(BF16)
    o_ref[...] = jnp.dot(sc, w_ref[...].astype(BF16), preferred_element_type=F32) + b_ref[...]


def _modulation(c_all, w_mod, b_mod):
    depth, D, N = w_mod.shape
    R = c_all.shape[0]
    tn = min(MOD_COL_TILE, N)
    assert N % tn == 0
    vmem = 2 * D * tn * 4 + 2 * R * tn * 4 + R * D * 4
    return pl.pallas_call(
        _mod_kernel,
        grid=(depth, N // tn),
        in_specs=[_resident((R, D), lambda i, j: (0, 0)),
                  pl.BlockSpec((None, D, tn), lambda i, j: (i, 0, j)),
                  pl.BlockSpec((None, 1, tn), lambda i, j: (i, 0, j))],
        out_specs=pl.BlockSpec((None, R, tn), lambda i, j: (i, 0, j)),
        out_shape=jax.ShapeDtypeStruct((depth, R, N), F32),
        compiler_params=_params(vmem, 2),
        name="adaln_modulation",
    )(c_all, w_mod, b_mod.reshape(depth, 1, N))


def _inproj_kernel(x_ref, g_ref, sc_ref, sh_ref, w_ref, *rest, with_gates):
    if with_gates:
        wg_ref, bg_ref, z_ref, gates_ref, h_scr = rest
    else:
        z_ref, h_scr = rest
    bt, tl, D = x_ref.shape
    rows = bt * tl

    @pl.when(pl.program_id(2) == 0)
    def _():
        h = _norm_mod(x_ref[...], g_ref[...], sc_ref[...], sh_ref[...])
        h2 = h.reshape(rows, D).astype(BF16)
        h_scr[...] = h2
        if with_gates:
            gates = jnp.dot(h2, wg_ref[...], preferred_element_type=F32) + bg_ref[...]
            gates_ref[...] = gates.reshape(gates_ref.shape)

    z = jnp.dot(h_scr[...], w_ref[...], preferred_element_type=F32)
    z_ref[...] = z.reshape(z_ref.shape).astype(z_ref.dtype)


def _inproj(x, g, sc, sh, w, w_gates=None, b_gates=None):
    B, L, D = x.shape
    N = w.shape[1]
    bt, tl = _row_blocking(B, L)
    rows = bt * tl
    tn = INPROJ_COL_TILE
    assert N % tn == 0
    with_gates = w_gates is not None
    row_map = lambda b, l, j: (b, l, 0)
    mod_map = lambda b, l, j: (b, 0, 0)
    in_specs = [pl.BlockSpec((bt, tl, D), row_map),
                _resident((1, D), lambda b, l, j: (0, 0)),
                pl.BlockSpec((bt, 1, D), mod_map),
                pl.BlockSpec((bt, 1, D), mod_map),
                pl.BlockSpec((D, tn), lambda b, l, j: (0, j))]
    args = [x, g, sc, sh, w]
    out_specs = [pl.BlockSpec((bt, tl, tn), lambda b, l, j: (b, l, j))]
    out_shape = [jax.ShapeDtypeStruct((B, L, N), BF16)]
    vmem = 2 * rows * D * 4 + 2 * D * tn * 2 + 2 * rows * tn * 2 + rows * D * 2
    if with_gates:
        in_specs += [_resident((D, LANES), lambda b, l, j: (0, 0)),
                     _resident((1, LANES), lambda b, l, j: (0, 0))]
        args += [w_gates, b_gates]
        out_specs.append(pl.BlockSpec((bt, tl, LANES), row_map))
        out_shape.append(jax.ShapeDtypeStruct((B, L, LANES), F32))
        vmem += D * LANES * 2 + 2 * rows * LANES * 4
    out = pl.pallas_call(
        functools.partial(_inproj_kernel, with_gates=with_gates),
        grid=(B // bt, L // tl, N // tn),
        in_specs=in_specs,
        out_specs=out_specs,
        out_shape=out_shape,
        scratch_shapes=[pltpu.VMEM((rows, D), BF16)],
        compiler_params=_params(vmem, 3),
        name="norm_mod_inproj",
    )(*args)
    return out if with_gates else out[0]


def _outproj_kernel(a_ref, x_ref, g_ref, gate_ref, w_ref, *rest, glu):
    if glu:
        w2_ref, o_ref = rest
    else:
        (o_ref,) = rest
    bt, tl, K = a_ref.shape
    a = a_ref[...].reshape(bt * tl, K)
    y = jnp.dot(a, w_ref[...], preferred_element_type=F32)
    if glu:
        y = y * jax.nn.sigmoid(jnp.dot(a, w2_ref[...], preferred_element_type=F32))
    yn = _rms_gain(y, g_ref[...])
    o_ref[...] = x_ref[...] + gate_ref[...] * yn.reshape(o_ref.shape)


def _outproj(a, x, g, gate, w, w2=None):
    B, L, D = x.shape
    K = a.shape[-1]
    glu = w2 is not None
    bt, tl = _row_blocking(B, L)
    if glu and tl == ROW_TILE:
        tl = ROW_TILE // 2
    rows = bt * tl
    row_map = lambda b, l: (b, l, 0)
    in_specs = [pl.BlockSpec((bt, tl, K), row_map),
                pl.BlockSpec((bt, tl, D), row_map),
                _resident((1, D), lambda b, l: (0, 0)),
                pl.BlockSpec((bt, 1, D), lambda b, l: (b, 0, 0)),
                _resident((K, D), lambda b, l: (0, 0))]
    args = [a, x, g, gate, w]
    vmem = 2 * rows * K * 2 + 4 * rows * D * 4 + K * D * 2 + 2 * rows * D * 4
    if glu:
        in_specs.append(_resident((K, D), lambda b, l: (0, 0)))
        args.append(w2)
        vmem += K * D * 2 + rows * D * 4
    return pl.pallas_call(
        functools.partial(_outproj_kernel, glu=glu),
        grid=(B // bt, L // tl),
        in_specs=in_specs,
        out_specs=pl.BlockSpec((bt, tl, D), row_map),
        out_shape=jax.ShapeDtypeStruct((B, L, D), F32),
        compiler_params=_params(vmem, 2),
        name="outproj_norm_residual",
    )(*args)


def _ffn_kernel(x_ref, g2_ref, sc_ref, sh_ref, g3_ref, gate_ref, wg_ref, wu_ref, wd_ref,
                o_ref, h_scr, acc_scr):
    bt, tl, D = x_ref.shape
    f = pl.program_id(2)

    @pl.when(f == 0)
    def _():
        h = _norm_mod(x_ref[...], g2_ref[...], sc_ref[...], sh_ref[...])
        h_scr[...] = h.reshape(bt * tl, D).astype(BF16)
        acc_scr[...] = jnp.zeros_like(acc_scr)

    h2 = h_scr[...]
    gg = jnp.dot(h2, wg_ref[...], preferred_element_type=F32)
    uu = jnp.dot(h2, wu_ref[...], preferred_element_type=F32)
    act = (gg * jax.nn.sigmoid(gg) * uu).astype(BF16)
    acc_scr[...] += jnp.dot(act, wd_ref[...], preferred_element_type=F32)

    @pl.when(f == pl.num_programs(2) - 1)
    def _():
        yn = _rms_gain(acc_scr[...], g3_ref[...])
        o_ref[...] = x_ref[...] + gate_ref[...] * yn.reshape(o_ref.shape)


def _ffn(x, g2, sc, sh, g3, gate, wg, wu, wd):
    B, L, D = x.shape
    F = wg.shape[1]
    bt, tl = _row_blocking(B, L)
    rows = bt * tl
    tf = FFN_COL_TILE
    assert F % tf == 0
    row_map = lambda b, l, f: (b, l, 0)
    mod_map = lambda b, l, f: (b, 0, 0)
    vec = lambda: _resident((1, D), lambda b, l, f: (0, 0))
    vmem = (4 * rows * D * 4 + 3 * 2 * D * tf * 2 + rows * D * 2 + rows * D * 4
            + 3 * rows * tf * 4)
    return pl.pallas_call(
        _ffn_kernel,
        grid=(B // bt, L // tl, F // tf),
        in_specs=[pl.BlockSpec((bt, tl, D), row_map), vec(),
                  pl.BlockSpec((bt, 1, D), mod_map), pl.BlockSpec((bt, 1, D), mod_map),
                  vec(), pl.BlockSpec((bt, 1, D), mod_map),
                  pl.BlockSpec((D, tf), lambda b, l, f: (0, f)),
                  pl.BlockSpec((D, tf), lambda b, l, f: (0, f)),
                  pl.BlockSpec((tf, D), lambda b, l, f: (f, 0))],
        out_specs=pl.BlockSpec((bt, tl, D), row_map),
        out_shape=jax.ShapeDtypeStruct((B, L, D), F32),
        scratch_shapes=[pltpu.VMEM((rows, D), BF16), pltpu.VMEM((rows, D), F32)],
        compiler_params=_params(vmem, 3),
        name="swiglu_ffn",
    )(x, g2, sc, sh, g3, gate, wg, wu, wd)


def _log_sigmoid(x):
    return -(jnp.maximum(-x, 0.0) + jnp.log1p(jnp.exp(-jnp.abs(x))))


def _cumsum_rows(x):
    n = x.shape[0]
    row = lax.broadcasted_iota(jnp.int32, x.shape, 0)
    s = 1
    while s < n:
        x = x + jnp.where(row >= s, pltpu.roll(x, s, 0), 0.0)
        s *= 2
    return x


def _mlstm_kernel(q_ref, k_ref, v_ref, o_ref, gt_ref, ghn_ref, c0_ref, n0_ref, m0_ref,
                  a_ref, c_ref, n_ref, m_ref, *, valid_len):
    H, DK, DV = c_ref.shape
    Lc = q_ref.shape[0]
    scale = DK ** -0.5

    @pl.when(pl.program_id(1) == 0)
    def _():
        c_ref[...] = c0_ref[...]
        n_ref[...] = n0_ref[...]
        m_ref[...] = m0_ref[...]

    gl = gt_ref[...]
    li_all = gl
    lf_all = _log_sigmoid(gl)
    if valid_len < Lc:
        valid = lax.broadcasted_iota(jnp.int32, gl.shape, 0) < valid_len
        li_all = jnp.where(valid, li_all, -jnp.inf)
        lf_all = jnp.where(valid, lf_all, 0.0)
    b_all = _cumsum_rows(lf_all)
    causal = (lax.broadcasted_iota(jnp.int32, (Lc, Lc), 0)
              >= lax.broadcasted_iota(jnp.int32, (Lc, Lc), 1))

    for h in range(H):
        q = q_ref[:, h * DK:(h + 1) * DK]
        k = k_ref[:, h * DK:(h + 1) * DK]
        v = v_ref[:, h * DV:(h + 1) * DV]
        b = b_all[:, H + h:H + h + 1]
        g = li_all[:, h:h + 1] - b
        g_row = jnp.transpose(jnp.broadcast_to(g, (Lc, LANES)))[0:1, :]
        m_prev = m_ref[h]
        C = c_ref[h]
        n = n_ref[h]

        dmat = jnp.where(causal, b + g_row, -jnp.inf)
        inter = b + m_prev
        m_t = jnp.maximum(inter, jnp.max(dmat, axis=-1, keepdims=True))
        w = jnp.exp(dmat - m_t)
        qk = lax.dot_general(q, k, (((1,), (1,)), ((), ())), preferred_element_type=F32)
        s = qk * scale * w
        inter_w = jnp.exp(inter - m_t) * scale
        qf = q.astype(F32)
        num = (jnp.dot(s.astype(BF16), v, preferred_element_type=F32)
               + inter_w * jnp.dot(q, C.astype(BF16), preferred_element_type=F32))
        den = (jnp.sum(s, axis=-1, keepdims=True)
               + inter_w * jnp.sum(qf * n, axis=-1, keepdims=True))
        floor = jnp.maximum(jnp.abs(den), jnp.exp(-m_t))
        hh = num * (1.0 / floor)

        m_new = m_t[Lc - 1:Lc, :]
        w_last = jnp.exp(b[Lc - 1:Lc, :] + g - m_new)
        decay = jnp.exp(inter[Lc - 1:Lc, :] - m_new)
        kw = k.astype(F32) * w_last
        c_ref[h] = decay * C + lax.dot_general(kw.astype(BF16), v, (((0,), (0,)), ((), ())),
                                               preferred_element_type=F32)
        n_ref[h] = decay * n + jnp.sum(kw, axis=0, keepdims=True)
        m_ref[h] = m_new

        hn = _rms_gain(hh, ghn_ref[:, h * DV:(h + 1) * DV])
        og = o_ref[:, h * DV:(h + 1) * DV].astype(F32)
        a_ref[:, h * DV:(h + 1) * DV] = (hn * jax.nn.sigmoid(og)).astype(a_ref.dtype)


def _mlstm_core(z, gates, ghn, C0, n0, m0):
    B, L, _ = z.shape
    _, H, DK, DV = C0.shape
    HK, HV = H * DK, H * DV
    assert HV == 2 * HK and z.shape[-1] == 2 * HK + 2 * HV
    valid_len = L
    if L >= MLSTM_CHUNK:
        Lc = MLSTM_CHUNK
        assert L % Lc == 0
    else:
        Lc = LANES
        z = jnp.pad(z, ((0, 0), (0, Lc - L), (0, 0)))
        gates = jnp.pad(gates, ((0, 0), (0, Lc - L), (0, 0)))
    Lp = z.shape[1]
    blk = lambda width, idx: pl.BlockSpec((None, Lc, width), lambda b, c: (b, c, idx))
    st3 = lambda d1, d2: pl.BlockSpec((None, H, d1, d2), lambda b, c: (b, 0, 0, 0))
    vmem = (2 * Lc * (2 * HK + 2 * HV) * 2 + 2 * Lc * LANES * 4 + 2 * Lc * HV * 2
            + 4 * H * DK * DV * 4 + 12 * Lc * Lc * 4 + 8 * Lc * DV * 4 + 2 * DK * DV * 4)
    a, C, n, m = pl.pallas_call(
        functools.partial(_mlstm_kernel, valid_len=valid_len),
        grid=(B, Lp // Lc),
        in_specs=[blk(HK, 0), blk(HK, 1), blk(HV, 1), blk(HV, 2), blk(LANES, 0),
                  _resident((1, HV), lambda b, c: (0, 0)),
                  st3(DK, DV), st3(1, DK), st3(1, 1)],
        out_specs=[blk(HV, 0), st3(DK, DV), st3(1, DK), st3(1, 1)],
        out_shape=[jax.ShapeDtypeStruct((B, Lp, HV), BF16),
                   jax.ShapeDtypeStruct((B, H, DK, DV), F32),
                   jax.ShapeDtypeStruct((B, H, 1, DK), F32),
                   jax.ShapeDtypeStruct((B, H, 1, 1), F32)],
        compiler_params=_params(vmem, 2),
        name="mlstm_chunk_scan",
    )(z, z, z, z, gates, ghn, C0, n0.reshape(B, H, 1, DK), m0.reshape(B, H, 1, 1))
    return a[:, :L], C, n.reshape(B, H, DK), m.reshape(B, H)


def _conv_kernel(gb_ref, gc_ref, u_ref, w_ref, prev_ref, a_ref, st_ref):
    tl, D = gc_ref.shape
    W = w_ref.shape[0]

    @pl.when(pl.program_id(1) == 0)
    def _():
        st_ref[...] = prev_ref[...]

    z = gc_ref[...].astype(F32) * u_ref[...].astype(F32)
    row = lax.broadcasted_iota(jnp.int32, (tl, D), 0)
    conv = z * w_ref[W - 1:W, :]
    for d in range(1, W):
        zd = pltpu.roll(z, d, 0)
        for r in range(d):
            zd = jnp.where(row == r, st_ref[W - 1 - d + r:W - d + r, :], zd)
        conv = conv + zd * w_ref[W - 1 - d:W - d, :]
    a_ref[...] = (gb_ref[...].astype(F32) * conv).astype(a_ref.dtype)
    st_ref[...] = z[tl - (W - 1):, :]


def _conv_core(z3, w_conv, prev):
    B, L, D3 = z3.shape
    D = D3 // 3
    W = w_conv.shape[1]
    tl = min(L, ROW_TILE)
    assert L % tl == 0 and tl >= W - 1
    blk = lambda idx: pl.BlockSpec((None, tl, D), lambda b, l: (b, l, idx))
    st = pl.BlockSpec((None, W - 1, D), lambda b, l: (b, 0, 0))
    vmem = 2 * 4 * tl * D * 2 + 6 * tl * D * 4
    return pl.pallas_call(
        _conv_kernel,
        grid=(B, L // tl),
        in_specs=[blk(0), blk(1), blk(2), _resident((W, D), lambda b, l: (0, 0)), st],
        out_specs=[pl.BlockSpec((None, tl, D), lambda b, l: (b, l, 0)), st],
        out_shape=[jax.ShapeDtypeStruct((B, L, D), BF16),
                   jax.ShapeDtypeStruct((B, W - 1, D), prev.dtype)],
        compiler_params=_params(vmem, 2),
        name="gated_short_conv",
    )(z3, z3, z3, jnp.transpose(w_conv), prev)


def _s5_disc_kernel(ar_ref, ai_ref, ldt_ref, br_ref, bi_ref, abr_ref, abi_ref, bbr_ref, bbi_ref):
    dt = jnp.exp(ldt_ref[...])
    lr, lim = ar_ref[...], ai_ref[...]
    mag = jnp.exp(lr * dt)
    ab_re, ab_im = mag * jnp.cos(lim * dt), mag * jnp.sin(lim * dt)
    den = lr * lr + lim * lim
    nr = ab_re - 1.0
    fr = (nr * lr + ab_im * lim) / den
    fi = (ab_im * lr - nr * lim) / den
    abr_ref[...] = ab_re
    abi_ref[...] = ab_im
    br, bi = br_ref[...], bi_ref[...]
    bbr_ref[...] = fr[:, None, :] * br - fi[:, None, :] * bi
    bbi_ref[...] = fr[:, None, :] * bi + fi[:, None, :] * br


def _s5_discretize(a_re, a_im, log_dt, b_re, b_im):
    G, N, P = b_re.shape
    full = lambda *shape: pl.BlockSpec(shape, lambda: (0,) * len(shape))
    return pl.pallas_call(
        _s5_disc_kernel,
        in_specs=[full(G, N), full(G, N), full(G, 1), full(G, P, N), full(G, P, N)],
        out_specs=[full(G, N), full(G, N), full(G, P, N), full(G, P, N)],
        out_shape=[jax.ShapeDtypeStruct((G, N), F32), jax.ShapeDtypeStruct((G, N), F32),
                   jax.ShapeDtypeStruct((G, P, N), F32), jax.ShapeDtypeStruct((G, P, N), F32)],
        name="s5_discretize",
    )(a_re, a_im, log_dt.reshape(G, 1), jnp.swapaxes(b_re, 1, 2), jnp.swapaxes(b_im, 1, 2))


def _s5_kernel(x_ref, g_ref, sc_ref, sh_ref, bm_ref, cm_ref, ar_ref, ai_ref, dsk_ref,
               s0r_ref, s0i_ref, y_ref, sr_ref, si_ref, h_scr, bu_scr):
    T, B, D = x_ref.shape
    KT, KW, SW2 = bm_ref.shape
    SW = SW2 // 2
    rows = T * B
    LG = S5_LANE_GROUP * LANES

    @pl.when(pl.program_id(0) == 0)
    def _():
        sr_ref[...] = s0r_ref[...]
        si_ref[...] = s0i_ref[...]

    h = _norm_mod(x_ref[...], g_ref[...], sc_ref[...], sh_ref[...])
    h_scr[...] = h.reshape(rows, D)

    for kt in range(KT):
        cols = slice(kt * KW, (kt + 1) * KW)
        hk = h_scr[:, cols]
        bu_scr[...] = jnp.dot(hk.astype(BF16), bm_ref[kt], preferred_element_type=F32)
        for lg in range(SW // LG):
            re_cols = slice(lg * LG, (lg + 1) * LG)
            im_cols = slice(SW + lg * LG, SW + (lg + 1) * LG)
            a_r = jnp.broadcast_to(ar_ref[kt, :, re_cols], (B, LG))
            a_i = jnp.broadcast_to(ai_ref[kt, :, re_cols], (B, LG))

            def step(t, carry, re_cols=re_cols, im_cols=im_cols, a_r=a_r, a_i=a_i):
                xr, xi = carry
                r = pl.ds(pl.multiple_of(t * B, B), B)
                nr = a_r * xr - a_i * xi + bu_scr[r, re_cols]
                ni = a_r * xi + a_i * xr + bu_scr[r, im_cols]
                bu_scr[r, re_cols] = nr
                bu_scr[r, im_cols] = ni
                return nr, ni

            xr, xi = lax.fori_loop(0, T, step, (sr_ref[kt, :, re_cols], si_ref[kt, :, re_cols]),
                                   unroll=4)
            sr_ref[kt, :, re_cols] = xr
            si_ref[kt, :, re_cols] = xi
        yk = jnp.dot(bu_scr[...].astype(BF16), cm_ref[kt], preferred_element_type=F32)
        yk = yk + dsk_ref[:, cols] * hk
        y_ref[:, :, cols] = jax.nn.gelu(yk).reshape(T, B, KW).astype(y_ref.dtype)


def _s5_core(xt, g, sc, sh, bmat, cmat, a_r, a_i, dsk, s0r, s0i):
    L, B, D = xt.shape
    KT, KW, SW2 = bmat.shape
    SW = SW2 // 2
    assert B == SUBLANES and SW % (S5_LANE_GROUP * LANES) == 0
    T = min(S5_CHUNK, L)
    assert L % T == 0
    rows = T * B
    c0 = lambda *shape: _resident(shape, lambda c: (0,) * len(shape))
    vmem = (2 * rows * D * 4 + 2 * rows * D * 2 + 2 * KT * KW * SW2 * 2 + rows * D * 4
            + 2 * rows * SW2 * 4 + 6 * KT * B * SW * 4)
    return pl.pallas_call(
        _s5_kernel,
        grid=(L // T,),
        in_specs=[pl.BlockSpec((T, B, D), lambda c: (c, 0, 0)), c0(1, D), c0(B, D), c0(B, D),
                  c0(KT, KW, SW2), c0(KT, SW2, KW), c0(KT, 1, SW), c0(KT, 1, SW), c0(1, D),
                  c0(KT, B, SW), c0(KT, B, SW)],
        out_specs=[pl.BlockSpec((T, B, D), lambda c: (c, 0, 0)),
                   pl.BlockSpec((KT, B, SW), lambda c: (0, 0, 0)),
                   pl.BlockSpec((KT, B, SW), lambda c: (0, 0, 0))],
        out_shape=[jax.ShapeDtypeStruct((L, B, D), BF16),
                   jax.ShapeDtypeStruct((KT, B, SW), F32),
                   jax.ShapeDtypeStruct((KT, B, SW), F32)],
        scratch_shapes=[pltpu.VMEM((rows, D), F32), pltpu.VMEM((rows, SW2), F32)],
        compiler_params=_params(vmem, 1),
        name="s5_scan",
    )(xt, g, sc, sh, bmat, cmat, a_r, a_i, dsk, s0r, s0i)


def _s5_weights(a_re, a_im, b_re, b_im, c_re, c_im, d_skip, log_dt):
    G, N, P = b_re.shape
    KW = MXU_DIM_V7X
    GP = KW // P
    KT = G // GP
    ab_re, ab_im, bb_re, bb_im = _s5_discretize(a_re, a_im, log_dt, b_re, b_im)
    eye = jnp.eye(GP, dtype=F32)
    bb = jnp.stack([bb_re, bb_im]).reshape(2, KT, GP, P, N)
    bmat = jnp.einsum('rkgpn,gh->kgprhn', bb, eye).reshape(KT, KW, 2 * GP * N)
    cc = jnp.stack([c_re, -c_im]).reshape(2, KT, GP, P, N)
    cmat = jnp.einsum('rkgpn,gh->krgnhp', cc, eye).reshape(KT, 2 * GP * N, KW)
    a_r = ab_re.reshape(KT, 1, GP * N)
    a_i = ab_im.reshape(KT, 1, GP * N)
    return bmat.astype(BF16), cmat.astype(BF16), a_r, a_i, d_skip.reshape(1, G * P)


def _trunk(x, mod, row0, st_C, st_n, st_m, st_conv, st_re, st_im, p):
    B, L, D = x.shape
    depth = p['g_norm'].shape[0]
    new_C, new_n, new_m, new_conv, new_re, new_im = [], [], [], [], [], []
    for i in range(depth):
        m6 = mod[i, row0:row0 + B].reshape(B, 6, 1, D)
        sh1, sc1, g1, sh2, sc2, g2 = (m6[:, j] for j in range(6))
        gn = p['g_norm'][i]
        gvec = lambda r: gn[r].reshape(1, D)
        kind, j = i % N_MIXERS, i // N_MIXERS
        if kind == 0:
            w = p['wA_in'][j]
            nz = w.shape[1] - 2 * st_C.shape[2]
            H = st_C.shape[2]
            w_gates = jnp.pad(w[:, nz:], ((0, 0), (0, LANES - 2 * H))).astype(BF16)
            b_gates = jnp.pad(p['bA_gates'][j], (0, LANES - 2 * H)).reshape(1, LANES)
            z, gates = _inproj(x, gvec(0), sc1, sh1, w[:, :nz].astype(BF16), w_gates, b_gates)
            a, C, n, m = _mlstm_core(z, gates, p['gA_hnorm'][j].reshape(1, -1),
                                     st_C[j], st_n[j], st_m[j])
            new_C.append(C); new_n.append(n); new_m.append(m)
            x = _outproj(a, x, gvec(1), g1, p['wA_out'][j].astype(BF16))
        elif kind == 1:
            z3 = _inproj(x, gvec(0), sc1, sh1, p['wB_in'][j].astype(BF16))
            a, cv = _conv_core(z3, p['wB_conv'][j], st_conv[j])
            new_conv.append(cv)
            x = _outproj(a, x, gvec(1), g1, p['wB_out'][j].astype(BF16))
        else:
            bmat, cmat, a_r, a_i, dsk = _s5_weights(
                p['s5_A_re'][j], p['s5_A_im'][j], p['s5_B_re'][j], p['s5_B_im'][j],
                p['s5_C_re'][j], p['s5_C_im'][j], p['s5_D'][j], p['s5_log_dt'][j])
            KT, _, SW2 = bmat.shape
            to_lanes = lambda s: jnp.swapaxes(s.reshape(B, KT, SW2 // 2), 0, 1)
            yt, sr, si = _s5_core(jnp.swapaxes(x, 0, 1), gvec(0), sc1.reshape(B, D),
                                  sh1.reshape(B, D), bmat, cmat, a_r, a_i, dsk,
                                  to_lanes(st_re[j]), to_lanes(st_im[j]))
            from_lanes = lambda s: jnp.swapaxes(s, 0, 1).reshape(st_re[j].shape)
            new_re.append(from_lanes(sr)); new_im.append(from_lanes(si))
            wc = p['wC_out'][j]
            x = _outproj(jnp.swapaxes(yt, 0, 1), x, gvec(1), g1,
                         wc[:, :D].astype(BF16), wc[:, D:].astype(BF16))
        x = _ffn(x, gvec(2), sc2, sh2, gvec(3), g2, p['w_ffn_gate'][i].astype(BF16),
                 p['w_ffn_up'][i].astype(BF16), p['w_ffn_down'][i].astype(BF16))
    return (x, jnp.stack(new_C), jnp.stack(new_n), jnp.stack(new_m), jnp.stack(new_conv),
            jnp.stack(new_re), jnp.stack(new_im))


def kernel(x_prompt, x_sample, state_mlstm_C, state_mlstm_n, state_mlstm_m, state_conv,
           state_s5_re, state_s5_im, c_prompt, c_sample, w_mod, b_mod, g_norm, wA_in,
           bA_gates, gA_hnorm, wA_out, wB_in, wB_conv, wB_out, s5_A_re, s5_A_im, s5_B_re,
           s5_B_im, s5_C_re, s5_C_im, s5_D, s5_log_dt, wC_out, w_ffn_gate, w_ffn_up,
           w_ffn_down):
    p = dict(g_norm=g_norm, wA_in=wA_in, bA_gates=bA_gates, gA_hnorm=gA_hnorm, wA_out=wA_out,
             wB_in=wB_in, wB_conv=wB_conv, wB_out=wB_out, s5_A_re=s5_A_re, s5_A_im=s5_A_im,
             s5_B_re=s5_B_re, s5_B_im=s5_B_im, s5_C_re=s5_C_re, s5_C_im=s5_C_im, s5_D=s5_D,
             s5_log_dt=s5_log_dt, wC_out=wC_out, w_ffn_gate=w_ffn_gate, w_ffn_up=w_ffn_up,
             w_ffn_down=w_ffn_down)
    bp = x_prompt.shape[0]
    mod = _modulation(jnp.concatenate([c_prompt, c_sample], axis=0), w_mod, b_mod)
    zeros = lambda s: jnp.zeros((s.shape[0], bp) + s.shape[2:], s.dtype)
    outs_p = _trunk(x_prompt, mod, 0, zeros(state_mlstm_C), zeros(state_mlstm_n),
                    zeros(state_mlstm_m), zeros(state_conv), zeros(state_s5_re),
                    zeros(state_s5_im), p)
    outs_s = _trunk(x_sample, mod, bp, state_mlstm_C, state_mlstm_n, state_mlstm_m,
                    state_conv, state_s5_re, state_s5_im, p)
    return (outs_p[0], outs_s[0]) + tuple(outs_p[1:]) + tuple(outs_s[1:])
```

```python
import functools

import jax
import jax.numpy as jnp
from jax import lax
from jax.experimental import pallas as pl
from jax.experimental.pallas import tpu as pltpu

F32 = jnp.float32
BF16 = jnp.bfloat16
EPS = 1e-6
N_MIXERS = 3

LANES = 128
SUBLANES = 8
MXU_DIM_V7X = 256
VMEM_BYTES_V7X = 64 * 1024 * 1024
VMEM_CAP = VMEM_BYTES_V7X - 6 * 1024 * 1024

ROW_TILE = 512
INPROJ_ROW_TILE = 1024
INPROJ_COL_TILE = 1024
FFN_COL_TILE = 512
MOD_COL_TILE = 1024
MLSTM_CHUNK = 256
S5_CHUNK = 32
S5_LANE_GROUP = 4


def _params(vmem_bytes, n_grid):
    limit = int(min(VMEM_CAP, max(vmem_bytes * 5 // 4 + (4 << 20), 16 << 20)))
    return pltpu.CompilerParams(dimension_semantics=("arbitrary",) * n_grid,
                                vmem_limit_bytes=limit)


def _resident(block_shape, index_map):
    return pl.BlockSpec(block_shape, index_map, pipeline_mode=pl.Buffered(1))


def _row_blocking(B, L, tile=ROW_TILE):
    if L >= ROW_TILE:
        tile = min(tile, L)
        assert L % tile == 0
        return 1, tile
    assert L % SUBLANES == 0
    return B, L


def _norm_mod(x, g, sc, sh):
    ms = jnp.mean(x * x, axis=-1, keepdims=True)
    y = x * lax.rsqrt(ms + EPS) * g
    return y * (1.0 + sc) + sh


def _rms_gain(y, g):
    ms = jnp.mean(y * y, axis=-1, keepdims=True)
    return y * lax.rsqrt(ms + EPS) * g


def _mod_kernel(c_ref, w_ref, b_ref, o_ref):
    c = c_ref[...]
    sc = (c * jax.nn.sigmoid(c)).astype(BF16)
    o_ref[...] = jnp.dot(sc, w_ref[...].astype(BF16), preferred_element_type=F32) + b_ref[...]


def _modulation(c_all, w_mod, b_mod):
    depth, D, N = w_mod.shape
    R = c_all.shape[0]
    tn = min(MOD_COL_TILE, N)
    assert N % tn == 0
    vmem = 2 * D * tn * 4 + 2 * R * tn * 4 + R * D * 4
    return pl.pallas_call(
        _mod_kernel,
        grid=(depth, N // tn),
        in_specs=[_resident((R, D), lambda i, j: (0, 0)),
                  pl.BlockSpec((None, D, tn), lambda i, j: (i, 0, j)),
                  pl.BlockSpec((None, 1, tn), lambda i, j: (i, 0, j))],
        out_specs=pl.BlockSpec((None, R, tn), lambda i, j: (i, 0, j)),
        out_shape=jax.ShapeDtypeStruct((depth, R, N), F32),
        compiler_params=_params(vmem, 2),
        name="adaln_modulation",
    )(c_all, w_mod, b_mod.reshape(depth, 1, N))


def _inproj_kernel(x_ref, g_ref, sc_ref, sh_ref, w_ref, *rest, with_gates):
    if with_gates:
        wg_ref, bg_ref, z_ref, gates_ref, h_scr = rest
    else:
        z_ref, h_scr = rest
    bt, tl, D = x_ref.shape
    rows = bt * tl

    @pl.when(pl.program_id(2) == 0)
    def _():
        h = _norm_mod(x_ref[...], g_ref[...], sc_ref[...], sh_ref[...])
        h2 = h.reshape(rows, D).astype(BF16)
        h_scr[...] = h2
        if with_gates:
            gates = jnp.dot(h2, wg_ref[...], preferred_element_type=F32) + bg_ref[...]
            gates_ref[...] = gates.reshape(gates_ref.shape)

    z = jnp.dot(h_scr[...], w_ref[...], preferred_element_type=F32)
    z_ref[...] = z.reshape(z_ref.shape).astype(z_ref.dtype)


def _inproj(x, g, sc, sh, w, w_gates=None, b_gates=None):
    B, L, D = x.shape
    N = w.shape[1]
    bt, tl = _row_blocking(B, L, INPROJ_ROW_TILE)
    rows = bt * tl
    tn = INPROJ_COL_TILE
    while N % tn:
        tn //= 2
    assert tn % LANES == 0
    with_gates = w_gates is not None
    row_map = lambda b, l, j: (b, l, 0)
    mod_map = lambda b, l, j: (b, 0, 0)
    in_specs = [pl.BlockSpec((bt, tl, D), row_map),
                _resident((1, D), lambda b, l, j: (0, 0)),
                pl.BlockSpec((bt, 1, D), mod_map),
                pl.BlockSpec((bt, 1, D), mod_map),
                pl.BlockSpec((D, tn), lambda b, l, j: (0, j))]
    args = [x, g, sc, sh, w]
    out_specs = [pl.BlockSpec((bt, tl, tn), lambda b, l, j: (b, l, j))]
    out_shape = [jax.ShapeDtypeStruct((B, L, N), BF16)]
    vmem = 2 * rows * D * 4 + 2 * D * tn * 2 + 2 * rows * tn * 2 + rows * D * 2
    if with_gates:
        in_specs += [_resident((D, LANES), lambda b, l, j: (0, 0)),
                     _resident((1, LANES), lambda b, l, j: (0, 0))]
        args += [w_gates, b_gates]
        out_specs.append(pl.BlockSpec((bt, tl, LANES), row_map))
        out_shape.append(jax.ShapeDtypeStruct((B, L, LANES), F32))
        vmem += D * LANES * 2 + 2 * rows * LANES * 4
    out = pl.pallas_call(
        functools.partial(_inproj_kernel, with_gates=with_gates),
        grid=(B // bt, L // tl, N // tn),
        in_specs=in_specs,
        out_specs=out_specs,
        out_shape=out_shape,
        scratch_shapes=[pltpu.VMEM((rows, D), BF16)],
        compiler_params=_params(vmem, 3),
        name="norm_mod_inproj",
    )(*args)
    return out if with_gates else out[0]


def _outproj_kernel(a_ref, x_ref, g_ref, gate_ref, w_ref, *rest, glu):
    if glu:
        w2_ref, o_ref = rest
    else:
        (o_ref,) = rest
    bt, tl, _ = x_ref.shape
    a = a_ref[...].reshape(bt * tl, a_ref.shape[-1])
    y = jnp.dot(a, w_ref[...], preferred_element_type=F32)
    if glu:
        y = y * jax.nn.sigmoid(jnp.dot(a, w2_ref[...], preferred_element_type=F32))
    yn = _rms_gain(y, g_ref[...])
    o_ref[...] = x_ref[...] + gate_ref[...] * yn.reshape(o_ref.shape)


def _outproj(a, x, g, gate, w, w2=None, a_time_major=False):
    B, L, D = x.shape
    K = a.shape[-1]
    glu = w2 is not None
    bt, tl = _row_blocking(B, L)
    rows = bt * tl
    row_map = lambda b, l: (b, l, 0)
    if a_time_major:
        assert bt == 1
        a = a.reshape(L, B * K)
        a_spec = pl.BlockSpec((tl, K), lambda b, l: (l, b))
    else:
        a_spec = pl.BlockSpec((bt, tl, K), row_map)
    in_specs = [a_spec,
                pl.BlockSpec((bt, tl, D), row_map),
                _resident((1, D), lambda b, l: (0, 0)),
                pl.BlockSpec((bt, 1, D), lambda b, l: (b, 0, 0)),
                _resident((K, D), lambda b, l: (0, 0))]
    args = [a, x, g, gate, w]
    vmem = 2 * rows * K * 2 + 4 * rows * D * 4 + K * D * 2 + 2 * rows * D * 4
    if glu:
        in_specs.append(_resident((K, D), lambda b, l: (0, 0)))
        args.append(w2)
        vmem += K * D * 2 + rows * D * 4
    return pl.pallas_call(
        functools.partial(_outproj_kernel, glu=glu),
        grid=(B // bt, L // tl),
        in_specs=in_specs,
        out_specs=pl.BlockSpec((bt, tl, D), row_map),
        out_shape=jax.ShapeDtypeStruct((B, L, D), F32),
        compiler_params=_params(vmem, 2),
        name="outproj_norm_residual",
    )(*args)


def _ffn_kernel(x_ref, g2_ref, sc_ref, sh_ref, g3_ref, gate_ref, wg_ref, wu_ref, wd_ref,
                o_ref, *rest):
    *ot_ref, h_scr, acc_scr = rest
    bt, tl, D = x_ref.shape
    f = pl.program_id(2)

    @pl.when(f == 0)
    def _():
        h = _norm_mod(x_ref[...], g2_ref[...], sc_ref[...], sh_ref[...])
        h_scr[...] = h.reshape(bt * tl, D).astype(BF16)
        acc_scr[...] = jnp.zeros_like(acc_scr)

    h2 = h_scr[...]
    gg = jnp.dot(h2, wg_ref[...], preferred_element_type=F32)
    uu = jnp.dot(h2, wu_ref[...], preferred_element_type=F32)
    act = (gg * jax.nn.sigmoid(gg) * uu).astype(BF16)
    acc_scr[...] += jnp.dot(act, wd_ref[...], preferred_element_type=F32)

    @pl.when(f == pl.num_programs(2) - 1)
    def _():
        yn = _rms_gain(acc_scr[...], g3_ref[...])
        out = x_ref[...] + gate_ref[...] * yn.reshape(o_ref.shape)
        o_ref[...] = out
        if ot_ref:
            ot_ref[0][...] = out.reshape(ot_ref[0].shape)


def _ffn(x, g2, sc, sh, g3, gate, wg, wu, wd, time_major_copy=False):
    B, L, D = x.shape
    F = wg.shape[1]
    bt, tl = _row_blocking(B, L)
    rows = bt * tl
    out_specs = [pl.BlockSpec((bt, tl, D), lambda b, l, f: (b, l, 0))]
    out_shape = [jax.ShapeDtypeStruct((B, L, D), F32)]
    if time_major_copy:
        assert bt == 1
        out_specs.append(pl.BlockSpec((tl, D), lambda b, l, f: (l, b)))
        out_shape.append(jax.ShapeDtypeStruct((L, B * D), F32))
    tf = FFN_COL_TILE
    assert F % tf == 0
    row_map = lambda b, l, f: (b, l, 0)
    mod_map = lambda b, l, f: (b, 0, 0)
    vec = lambda: _resident((1, D), lambda b, l, f: (0, 0))
    vmem = ((4 + 2 * time_major_copy) * rows * D * 4 + 3 * 2 * D * tf * 2 + rows * D * 2
            + rows * D * 4 + 3 * rows * tf * 4)
    out = pl.pallas_call(
        _ffn_kernel,
        grid=(B // bt, L // tl, F // tf),
        in_specs=[pl.BlockSpec((bt, tl, D), row_map), vec(),
                  pl.BlockSpec((bt, 1, D), mod_map), pl.BlockSpec((bt, 1, D), mod_map),
                  vec(), pl.BlockSpec((bt, 1, D), mod_map),
                  pl.BlockSpec((D, tf), lambda b, l, f: (0, f)),
                  pl.BlockSpec((D, tf), lambda b, l, f: (0, f)),
                  pl.BlockSpec((tf, D), lambda b, l, f: (f, 0))],
        out_specs=out_specs,
        out_shape=out_shape,
        scratch_shapes=[pltpu.VMEM((rows, D), BF16), pltpu.VMEM((rows, D), F32)],
        compiler_params=_params(vmem, 3),
        name="swiglu_ffn",
    )(x, g2, sc, sh, g3, gate, wg, wu, wd)
    if time_major_copy:
        return out[0], out[1].reshape(L, B, D)
    return out[0]


def _log_sigmoid(x):
    return -(jnp.maximum(-x, 0.0) + jnp.log1p(jnp.exp(-jnp.abs(x))))


def _cumsum_rows(x):
    n = x.shape[0]
    row = lax.broadcasted_iota(jnp.int32, x.shape, 0)
    s = 1
    while s < n:
        x = x + jnp.where(row >= s, pltpu.roll(x, s, 0), 0.0)
        s *= 2
    return x


def _mlstm_kernel(q_ref, k_ref, v_ref, o_ref, gt_ref, ghn_ref, c0_ref, n0_ref, m0_ref,
                  a_ref, c_ref, n_ref, m_ref, *, valid_len):
    H, DK, DV = c_ref.shape
    Lc = q_ref.shape[0]
    scale = DK ** -0.5

    @pl.when(pl.program_id(1) == 0)
    def _():
        c_ref[...] = c0_ref[...]
        n_ref[...] = n0_ref[...]
        m_ref[...] = m0_ref[...]

    gl = gt_ref[...]
    li_all = gl
    lf_all = _log_sigmoid(gl)
    if valid_len < Lc:
        valid = lax.broadcasted_iota(jnp.int32, gl.shape, 0) < valid_len
        li_all = jnp.where(valid, li_all, -jnp.inf)
        lf_all = jnp.where(valid, lf_all, 0.0)
    b_all = _cumsum_rows(lf_all)
    causal = (lax.broadcasted_iota(jnp.int32, (Lc, Lc), 0)
              >= lax.broadcasted_iota(jnp.int32, (Lc, Lc), 1))

    for h in range(H):
        q = q_ref[:, h * DK:(h + 1) * DK]
        k = k_ref[:, h * DK:(h + 1) * DK]
        v = v_ref[:, h * DV:(h + 1) * DV]
        b = b_all[:, H + h:H + h + 1]
        g = li_all[:, h:h + 1] - b
        g_row = jnp.transpose(jnp.broadcast_to(g, (Lc, LANES)))[0:1, :]
        m_prev = m_ref[h]
        C = c_ref[h]
        n = n_ref[h]

        dmat = jnp.where(causal, b + g_row, -jnp.inf)
        inter = b + m_prev
        m_t = jnp.maximum(inter, jnp.max(dmat, axis=-1, keepdims=True))
        w = jnp.exp(dmat - m_t)
        qk = lax.dot_general(q, k, (((1,), (1,)), ((), ())), preferred_element_type=F32)
        s = qk * scale * w
        inter_w = jnp.exp(inter - m_t) * scale
        qf = q.astype(F32)
        num = (jnp.dot(s.astype(BF16), v, preferred_element_type=F32)
               + inter_w * jnp.dot(q, C.astype(BF16), preferred_element_type=F32))
        den = (jnp.sum(s, axis=-1, keepdims=True)
               + inter_w * jnp.sum(qf * n, axis=-1, keepdims=True))
        floor = jnp.maximum(jnp.abs(den), jnp.exp(-m_t))
        hh = num * (1.0 / floor)

        m_new = m_t[Lc - 1:Lc, :]
        w_last = jnp.exp(b[Lc - 1:Lc, :] + g - m_new)
        decay = jnp.exp(inter[Lc - 1:Lc, :] - m_new)
        kw = k.astype(F32) * w_last
        c_ref[h] = decay * C + lax.dot_general(kw.astype(BF16), v, (((0,), (0,)), ((), ())),
                                               preferred_element_type=F32)
        n_ref[h] = decay * n + jnp.sum(kw, axis=0, keepdims=True)
        m_ref[h] = m_new

        hn = _rms_gain(hh, ghn_ref[:, h * DV:(h + 1) * DV])
        og = o_ref[:, h * DV:(h + 1) * DV].astype(F32)
        a_ref[:, h * DV:(h + 1) * DV] = (hn * jax.nn.sigmoid(og)).astype(a_ref.dtype)


def _mlstm_core(z, gates, ghn, C0, n0, m0):
    B, L, _ = z.shape
    _, H, DK, DV = C0.shape
    HK, HV = H * DK, H * DV
    assert HV == 2 * HK and z.shape[-1] == 2 * HK + 2 * HV
    valid_len = L
    if L >= MLSTM_CHUNK:
        Lc = MLSTM_CHUNK
        assert L % Lc == 0
    else:
        Lc = LANES
        z = jnp.pad(z, ((0, 0), (0, Lc - L), (0, 0)))
        gates = jnp.pad(gates, ((0, 0), (0, Lc - L), (0, 0)))
    Lp = z.shape[1]
    blk = lambda width, idx: pl.BlockSpec((None, Lc, width), lambda b, c: (b, c, idx))
    st3 = lambda d1, d2: pl.BlockSpec((None, H, d1, d2), lambda b, c: (b, 0, 0, 0))
    vmem = (2 * Lc * (2 * HK + 2 * HV) * 2 + 2 * Lc * LANES * 4 + 2 * Lc * HV * 2
            + 4 * H * DK * DV * 4 + 12 * Lc * Lc * 4 + 8 * Lc * DV * 4 + 2 * DK * DV * 4)
    a, C, n, m = pl.pallas_call(
        functools.partial(_mlstm_kernel, valid_len=valid_len),
        grid=(B, Lp // Lc),
        in_specs=[blk(HK, 0), blk(HK, 1), blk(HV, 1), blk(HV, 2), blk(LANES, 0),
                  _resident((1, HV), lambda b, c: (0, 0)),
                  st3(DK, DV), st3(1, DK), st3(1, 1)],
        out_specs=[blk(HV, 0), st3(DK, DV), st3(1, DK), st3(1, 1)],
        out_shape=[jax.ShapeDtypeStruct((B, Lp, HV), BF16),
                   jax.ShapeDtypeStruct((B, H, DK, DV), F32),
                   jax.ShapeDtypeStruct((B, H, 1, DK), F32),
                   jax.ShapeDtypeStruct((B, H, 1, 1), F32)],
        compiler_params=_params(vmem, 2),
        name="mlstm_chunk_scan",
    )(z, z, z, z, gates, ghn, C0, n0.reshape(B, H, 1, DK), m0.reshape(B, H, 1, 1))
    return a[:, :L], C, n.reshape(B, H, DK), m.reshape(B, H)


def _conv_kernel(gb_ref, gc_ref, u_ref, w_ref, prev_ref, a_ref, st_ref):
    tl, D = gc_ref.shape
    W = w_ref.shape[0]

    @pl.when(pl.program_id(1) == 0)
    def _():
        st_ref[...] = prev_ref[...]

    z = gc_ref[...].astype(F32) * u_ref[...].astype(F32)
    row = lax.broadcasted_iota(jnp.int32, (tl, D), 0)
    conv = z * w_ref[W - 1:W, :]
    for d in range(1, W):
        zd = pltpu.roll(z, d, 0)
        for r in range(d):
            zd = jnp.where(row == r, st_ref[W - 1 - d + r:W - d + r, :], zd)
        conv = conv + zd * w_ref[W - 1 - d:W - d, :]
    a_ref[...] = (gb_ref[...].astype(F32) * conv).astype(a_ref.dtype)
    st_ref[...] = z[tl - (W - 1):, :]


def _conv_core(z3, w_conv, prev):
    B, L, D3 = z3.shape
    D = D3 // 3
    W = w_conv.shape[1]
    tl = min(L, ROW_TILE)
    assert L % tl == 0 and tl >= W - 1
    blk = lambda idx: pl.BlockSpec((None, tl, D), lambda b, l: (b, l, idx))
    st = pl.BlockSpec((None, W - 1, D), lambda b, l: (b, 0, 0))
    vmem = 2 * 4 * tl * D * 2 + 6 * tl * D * 4
    return pl.pallas_call(
        _conv_kernel,
        grid=(B, L // tl),
        in_specs=[blk(0), blk(1), blk(2), _resident((W, D), lambda b, l: (0, 0)), st],
        out_specs=[pl.BlockSpec((None, tl, D), lambda b, l: (b, l, 0)), st],
        out_shape=[jax.ShapeDtypeStruct((B, L, D), BF16),
                   jax.ShapeDtypeStruct((B, W - 1, D), prev.dtype)],
        compiler_params=_params(vmem, 2),
        name="gated_short_conv",
    )(z3, z3, z3, jnp.transpose(w_conv), prev)


def _s5_disc_kernel(ar_ref, ai_ref, ldt_ref, br_ref, bi_ref, abr_ref, abi_ref, bbr_ref, bbi_ref):
    dt = jnp.exp(ldt_ref[...])
    lr, lim = ar_ref[...], ai_ref[...]
    mag = jnp.exp(lr * dt)
    ab_re, ab_im = mag * jnp.cos(lim * dt), mag * jnp.sin(lim * dt)
    den = lr * lr + lim * lim
    nr = ab_re - 1.0
    fr = (nr * lr + ab_im * lim) / den
    fi = (ab_im * lr - nr * lim) / den
    abr_ref[...] = ab_re
    abi_ref[...] = ab_im
    br, bi = br_ref[...], bi_ref[...]
    bbr_ref[...] = fr[:, None, :] * br - fi[:, None, :] * bi
    bbi_ref[...] = fr[:, None, :] * bi + fi[:, None, :] * br


def _s5_discretize(a_re, a_im, log_dt, b_re, b_im):
    G, N, P = b_re.shape
    full = lambda *shape: pl.BlockSpec(shape, lambda: (0,) * len(shape))
    return pl.pallas_call(
        _s5_disc_kernel,
        in_specs=[full(G, N), full(G, N), full(G, 1), full(G, P, N), full(G, P, N)],
        out_specs=[full(G, N), full(G, N), full(G, P, N), full(G, P, N)],
        out_shape=[jax.ShapeDtypeStruct((G, N), F32), jax.ShapeDtypeStruct((G, N), F32),
                   jax.ShapeDtypeStruct((G, P, N), F32), jax.ShapeDtypeStruct((G, P, N), F32)],
        name="s5_discretize",
    )(a_re, a_im, log_dt.reshape(G, 1), jnp.swapaxes(b_re, 1, 2), jnp.swapaxes(b_im, 1, 2))


def _s5_kernel(x_ref, g_ref, sc_ref, sh_ref, bm_ref, cm_ref, ar_ref, ai_ref, dsk_ref,
               s0r_ref, s0i_ref, y_ref, sr_ref, si_ref, h_scr, bu_scr):
    T, B, D = x_ref.shape
    KT, KW, SW2 = bm_ref.shape
    SW = SW2 // 2
    rows = T * B
    LG = S5_LANE_GROUP * LANES

    @pl.when(pl.program_id(0) == 0)
    def _():
        sr_ref[...] = s0r_ref[...]
        si_ref[...] = s0i_ref[...]

    h = _norm_mod(x_ref[...], g_ref[...], sc_ref[...], sh_ref[...])
    h_scr[...] = h.reshape(rows, D)

    for kt in range(KT):
        cols = slice(kt * KW, (kt + 1) * KW)
        bu = bu_scr.at[kt % 2]
        hk = h_scr[:, cols]
        bu[...] = jnp.dot(hk.astype(BF16), bm_ref[kt], preferred_element_type=F32)
        for lg in range(SW // LG):
            re_cols = slice(lg * LG, (lg + 1) * LG)
            im_cols = slice(SW + lg * LG, SW + (lg + 1) * LG)
            a_r = jnp.broadcast_to(ar_ref[kt, :, re_cols], (B, LG))
            a_i = jnp.broadcast_to(ai_ref[kt, :, re_cols], (B, LG))
            xr, xi = sr_ref[kt, :, re_cols], si_ref[kt, :, re_cols]
            for t in range(T):
                r = slice(t * B, (t + 1) * B)
                xr, xi = (a_r * xr - a_i * xi + bu[r, re_cols],
                          a_r * xi + a_i * xr + bu[r, im_cols])
                bu[r, re_cols] = xr
                bu[r, im_cols] = xi
            sr_ref[kt, :, re_cols] = xr
            si_ref[kt, :, re_cols] = xi
        yk = jnp.dot(bu[...].astype(BF16), cm_ref[kt], preferred_element_type=F32)
        yk = yk + dsk_ref[:, cols] * hk
        y_ref[:, :, cols] = jax.nn.gelu(yk).reshape(T, B, KW).astype(y_ref.dtype)


def _s5_core(xt, g, sc, sh, bmat, cmat, a_r, a_i, dsk, s0r, s0i):
    L, B, D = xt.shape
    KT, KW, SW2 = bmat.shape
    SW = SW2 // 2
    assert B == SUBLANES and SW % (S5_LANE_GROUP * LANES) == 0
    T = min(S5_CHUNK, L)
    assert L % T == 0
    rows = T * B
    c0 = lambda *shape: _resident(shape, lambda c: (0,) * len(shape))
    vmem = (2 * rows * D * 4 + 2 * rows * D * 2 + 2 * KT * KW * SW2 * 2 + rows * D * 4
            + 3 * rows * SW2 * 4 + 6 * KT * B * SW * 4)
    return pl.pallas_call(
        _s5_kernel,
        grid=(L // T,),
        in_specs=[pl.BlockSpec((T, B, D), lambda c: (c, 0, 0)), c0(1, D), c0(B, D), c0(B, D),
                  c0(KT, KW, SW2), c0(KT, SW2, KW), c0(KT, 1, SW), c0(KT, 1, SW), c0(1, D),
                  c0(KT, B, SW), c0(KT, B, SW)],
        out_specs=[pl.BlockSpec((T, B, D), lambda c: (c, 0, 0)),
                   pl.BlockSpec((KT, B, SW), lambda c: (0, 0, 0)),
                   pl.BlockSpec((KT, B, SW), lambda c: (0, 0, 0))],
        out_shape=[jax.ShapeDtypeStruct((L, B, D), BF16),
                   jax.ShapeDtypeStruct((KT, B, SW), F32),
                   jax.ShapeDtypeStruct((KT, B, SW), F32)],
        scratch_shapes=[pltpu.VMEM((rows, D), F32), pltpu.VMEM((2, rows, SW2), F32)],
        compiler_params=_params(vmem, 1),
        name="s5_scan",
    )(xt, g, sc, sh, bmat, cmat, a_r, a_i, dsk, s0r, s0i)


def _s5_weights(a_re, a_im, b_re, b_im, c_re, c_im, d_skip, log_dt):
    G, N, P = b_re.shape
    KW = MXU_DIM_V7X
    GP = KW // P
    KT = G // GP
    ab_re, ab_im, bb_re, bb_im = _s5_discretize(a_re, a_im, log_dt, b_re, b_im)
    eye = jnp.eye(GP, dtype=F32)
    bb = jnp.stack([bb_re, bb_im]).reshape(2, KT, GP, P, N)
    bmat = jnp.einsum('rkgpn,gh->kgprhn', bb, eye).reshape(KT, KW, 2 * GP * N)
    cc = jnp.stack([c_re, -c_im]).reshape(2, KT, GP, P, N)
    cmat = jnp.einsum('rkgpn,gh->krgnhp', cc, eye).reshape(KT, 2 * GP * N, KW)
    a_r = ab_re.reshape(KT, 1, GP * N)
    a_i = ab_im.reshape(KT, 1, GP * N)
    return bmat.astype(BF16), cmat.astype(BF16), a_r, a_i, d_skip.reshape(1, G * P)


def _trunk(x, mod, row0, st_C, st_n, st_m, st_conv, st_re, st_im, p):
    B, L, D = x.shape
    depth = p['g_norm'].shape[0]
    new_C, new_n, new_m, new_conv, new_re, new_im = [], [], [], [], [], []
    tiled = L >= ROW_TILE
    xt = None
    for i in range(depth):
        m6 = mod[i, row0:row0 + B].reshape(B, 6, 1, D)
        sh1, sc1, g1, sh2, sc2, g2 = (m6[:, j] for j in range(6))
        gn = p['g_norm'][i]
        gvec = lambda r: gn[r].reshape(1, D)
        kind, j = i % N_MIXERS, i // N_MIXERS
        if kind == 0:
            w = p['wA_in'][j]
            nz = w.shape[1] - 2 * st_C.shape[2]
            H = st_C.shape[2]
            w_gates = jnp.pad(w[:, nz:], ((0, 0), (0, LANES - 2 * H))).astype(BF16)
            b_gates = jnp.pad(p['bA_gates'][j], (0, LANES - 2 * H)).reshape(1, LANES)
            z, gates = _inproj(x, gvec(0), sc1, sh1, w[:, :nz].astype(BF16), w_gates, b_gates)
            a, C, n, m = _mlstm_core(z, gates, p['gA_hnorm'][j].reshape(1, -1),
                                     st_C[j], st_n[j], st_m[j])
            new_C.append(C); new_n.append(n); new_m.append(m)
            x = _outproj(a, x, gvec(1), g1, p['wA_out'][j].astype(BF16))
        elif kind == 1:
            z3 = _inproj(x, gvec(0), sc1, sh1, p['wB_in'][j].astype(BF16))
            a, cv = _conv_core(z3, p['wB_conv'][j], st_conv[j])
            new_conv.append(cv)
            x = _outproj(a, x, gvec(1), g1, p['wB_out'][j].astype(BF16))
        else:
            bmat, cmat, a_r, a_i, dsk = _s5_weights(
                p['s5_A_re'][j], p['s5_A_im'][j], p['s5_B_re'][j], p['s5_B_im'][j],
                p['s5_C_re'][j], p['s5_C_im'][j], p['s5_D'][j], p['s5_log_dt'][j])
            KT, _, SW2 = bmat.shape
            to_lanes = lambda s: jnp.swapaxes(s.reshape(B, KT, SW2 // 2), 0, 1)
            if xt is None:
                xt = jnp.swapaxes(x, 0, 1)
            yt, sr, si = _s5_core(xt, gvec(0), sc1.reshape(B, D),
                                  sh1.reshape(B, D), bmat, cmat, a_r, a_i, dsk,
                                  to_lanes(st_re[j]), to_lanes(st_im[j]))
            from_lanes = lambda s: jnp.swapaxes(s, 0, 1).reshape(st_re[j].shape)
            new_re.append(from_lanes(sr)); new_im.append(from_lanes(si))
            wc = p['wC_out'][j]
            x = _outproj(yt if tiled else jnp.swapaxes(yt, 0, 1), x, gvec(1), g1,
                         wc[:, :D].astype(BF16), wc[:, D:].astype(BF16), a_time_major=tiled)
        want_xt = tiled and i + 1 < depth and (i + 1) % N_MIXERS == 2
        x = _ffn(x, gvec(2), sc2, sh2, gvec(3), g2, p['w_ffn_gate'][i].astype(BF16),
                 p['w_ffn_up'][i].astype(BF16), p['w_ffn_down'][i].astype(BF16),
                 time_major_copy=want_xt)
        xt = None
        if want_xt:
            x, xt = x
    return (x, jnp.stack(new_C), jnp.stack(new_n), jnp.stack(new_m), jnp.stack(new_conv),
            jnp.stack(new_re), jnp.stack(new_im))


def kernel(x_prompt, x_sample, state_mlstm_C, state_mlstm_n, state_mlstm_m, state_conv,
           state_s5_re, state_s5_im, c_prompt, c_sample, w_mod, b_mod, g_norm, wA_in,
           bA_gates, gA_hnorm, wA_out, wB_in, wB_conv, wB_out, s5_A_re, s5_A_im, s5_B_re,
           s5_B_im, s5_C_re, s5_C_im, s5_D, s5_log_dt, wC_out, w_ffn_gate, w_ffn_up,
           w_ffn_down):
    p = dict(g_norm=g_norm, wA_in=wA_in, bA_gates=bA_gates, gA_hnorm=gA_hnorm, wA_out=wA_out,
             wB_in=wB_in, wB_conv=wB_conv, wB_out=wB_out, s5_A_re=s5_A_re, s5_A_im=s5_A_im,
             s5_B_re=s5_B_re, s5_B_im=s5_B_im, s5_C_re=s5_C_re, s5_C_im=s5_C_im, s5_D=s5_D,
             s5_log_dt=s5_log_dt, wC_out=wC_out, w_ffn_gate=w_ffn_gate, w_ffn_up=w_ffn_up,
             w_ffn_down=w_ffn_down)
    bp = x_prompt.shape[0]
    mod = _modulation(jnp.concatenate([c_prompt, c_sample], axis=0), w_mod, b_mod)
    zeros = lambda s: jnp.zeros((s.shape[0], bp) + s.shape[2:], s.dtype)
    outs_p = _trunk(x_prompt, mod, 0, zeros(state_mlstm_C), zeros(state_mlstm_n),
                    zeros(state_mlstm_m), zeros(state_conv), zeros(state_s5_re),
                    zeros(state_s5_im), p)
    outs_s = _trunk(x_sample, mod, bp, state_mlstm_C, state_mlstm_n, state_mlstm_m,
                    state_conv, state_s5_re, state_s5_im, p)
    return (outs_p[0], outs_s[0]) + tuple(outs_p[1:]) + tuple(outs_s[1:])
```

```python
import functools

import jax
import jax.numpy as jnp
from jax import lax
from jax.experimental import pallas as pl
from jax.experimental.pallas import tpu as pltpu

F32 = jnp.float32
BF16 = jnp.bfloat16
EPS = 1e-6
N_MIXERS = 3

LANES = 128
SUBLANES = 8
MXU_DIM_V7X = 256
VMEM_BYTES_V7X = 64 * 1024 * 1024
VMEM_CAP = VMEM_BYTES_V7X - 6 * 1024 * 1024

ROW_TILE = 512
ROW_GROUP = MXU_DIM_V7X
FFN_ROW_TILE = 1024
INPROJ_ROW_TILE = 1024
INPROJ_COL_TILE = 1024
FFN_COL_TILE = 512
MOD_COL_TILE = 1024
MLSTM_CHUNK = 256
S5_CHUNK = 32
S5_LANE_GROUP = 4


def _params(vmem_bytes, n_grid):
    limit = int(min(VMEM_CAP, max(vmem_bytes * 5 // 4 + (4 << 20), 16 << 20)))
    return pltpu.CompilerParams(dimension_semantics=("arbitrary",) * n_grid,
                                vmem_limit_bytes=limit)


def _resident(block_shape, index_map):
    return pl.BlockSpec(block_shape, index_map, pipeline_mode=pl.Buffered(1))


def _row_blocking(B, L, tile=ROW_TILE):
    if L >= ROW_TILE:
        tile = min(tile, L)
        assert L % tile == 0
        return 1, tile
    assert L % SUBLANES == 0
    return B, L


def _norm_mod(x, g, sc, sh):
    ms = jnp.mean(x * x, axis=-1, keepdims=True)
    y = x * lax.rsqrt(ms + EPS) * g
    return y * (1.0 + sc) + sh


def _rms_gain(y, g):
    ms = jnp.mean(y * y, axis=-1, keepdims=True)
    return y * lax.rsqrt(ms + EPS) * g


def _row_groups(bt, tl):
    if bt != 1 or tl <= ROW_GROUP:
        return [(slice(0, tl), slice(0, bt * tl))]
    assert tl % ROW_GROUP == 0
    return [(slice(q * ROW_GROUP, (q + 1) * ROW_GROUP),) * 2 for q in range(tl // ROW_GROUP)]


def _mod_kernel(c_ref, w_ref, b_ref, o_ref):
    c = c_ref[...]
    sc = (c * jax.nn.sigmoid(c)).astype(BF16)
    o_ref[...] = jnp.dot(sc, w_ref[...].astype(BF16), preferred_element_type=F32) + b_ref[...]


def _modulation(c_all, w_mod, b_mod):
    depth, D, N = w_mod.shape
    R = c_all.shape[0]
    tn = min(MOD_COL_TILE, N)
    assert N % tn == 0
    vmem = 2 * D * tn * 4 + 2 * R * tn * 4 + R * D * 4
    return pl.pallas_call(
        _mod_kernel,
        grid=(depth, N // tn),
        in_specs=[_resident((R, D), lambda i, j: (0, 0)),
                  pl.BlockSpec((None, D, tn), lambda i, j: (i, 0, j)),
                  pl.BlockSpec((None, 1, tn), lambda i, j: (i, 0, j))],
        out_specs=pl.BlockSpec((None, R, tn), lambda i, j: (i, 0, j)),
        out_shape=jax.ShapeDtypeStruct((depth, R, N), F32),
        compiler_params=_params(vmem, 2),
        name="adaln_modulation",
    )(c_all, w_mod, b_mod.reshape(depth, 1, N))


def _inproj_kernel(x_ref, g_ref, sc_ref, sh_ref, w_ref, *rest, with_gates):
    if with_gates:
        wg_ref, bg_ref, z_ref, gates_ref, h_scr = rest
    else:
        z_ref, h_scr = rest
    bt, tl, D = x_ref.shape
    j = pl.program_id(2)

    @pl.when(j == 0)
    def _():
        for tsl, fr in _row_groups(bt, tl):
            h = _norm_mod(x_ref[:, tsl, :], g_ref[...], sc_ref[...], sh_ref[...])
            h2 = h.reshape(-1, D).astype(BF16)
            h_scr[fr, :] = h2
            z = jnp.dot(h2, w_ref[...], preferred_element_type=F32)
            z_ref[:, tsl, :] = z.reshape(bt, -1, z.shape[-1]).astype(z_ref.dtype)
            if with_gates:
                gates = jnp.dot(h2, wg_ref[...], preferred_element_type=F32) + bg_ref[...]
                gates_ref[:, tsl, :] = gates.reshape(bt, -1, gates.shape[-1])

    @pl.when(j > 0)
    def _():
        z = jnp.dot(h_scr[...], w_ref[...], preferred_element_type=F32)
        z_ref[...] = z.reshape(z_ref.shape).astype(z_ref.dtype)


def _inproj(x, g, sc, sh, w, w_gates=None, b_gates=None):
    B, L, D = x.shape
    N = w.shape[1]
    bt, tl = _row_blocking(B, L, INPROJ_ROW_TILE)
    rows = bt * tl
    tn = INPROJ_COL_TILE
    while N % tn:
        tn //= 2
    assert tn % LANES == 0
    with_gates = w_gates is not None
    row_map = lambda b, l, j: (b, l, 0)
    mod_map = lambda b, l, j: (b, 0, 0)
    in_specs = [pl.BlockSpec((bt, tl, D), row_map),
                _resident((1, D), lambda b, l, j: (0, 0)),
                pl.BlockSpec((bt, 1, D), mod_map),
                pl.BlockSpec((bt, 1, D), mod_map),
                pl.BlockSpec((D, tn), lambda b, l, j: (0, j))]
    args = [x, g, sc, sh, w]
    out_specs = [pl.BlockSpec((bt, tl, tn), lambda b, l, j: (b, l, j))]
    out_shape = [jax.ShapeDtypeStruct((B, L, N), BF16)]
    vmem = 2 * rows * D * 4 + 2 * D * tn * 2 + 2 * rows * tn * 2 + rows * D * 2
    if with_gates:
        in_specs += [_resident((D, LANES), lambda b, l, j: (0, 0)),
                     _resident((1, LANES), lambda b, l, j: (0, 0))]
        args += [w_gates, b_gates]
        out_specs.append(pl.BlockSpec((bt, tl, LANES), row_map))
        out_shape.append(jax.ShapeDtypeStruct((B, L, LANES), F32))
        vmem += D * LANES * 2 + 2 * rows * LANES * 4
    out = pl.pallas_call(
        functools.partial(_inproj_kernel, with_gates=with_gates),
        grid=(B // bt, L // tl, N // tn),
        in_specs=in_specs,
        out_specs=out_specs,
        out_shape=out_shape,
        scratch_shapes=[pltpu.VMEM((rows, D), BF16)],
        compiler_params=_params(vmem, 3),
        name="norm_mod_inproj",
    )(*args)
    return out if with_gates else out[0]


def _outproj_kernel(a_ref, x_ref, g_ref, gate_ref, w_ref, *rest, glu):
    if glu:
        w2_ref, o_ref = rest
    else:
        (o_ref,) = rest
    bt, tl, D = x_ref.shape
    for tsl, fr in _row_groups(bt, tl):
        a = a_ref[fr, :] if len(a_ref.shape) == 2 else a_ref[:, tsl, :].reshape(-1, a_ref.shape[-1])
        y = jnp.dot(a, w_ref[...], preferred_element_type=F32)
        if glu:
            y = y * jax.nn.sigmoid(jnp.dot(a, w2_ref[...], preferred_element_type=F32))
        yn = _rms_gain(y, g_ref[...]).reshape(bt, -1, D)
        o_ref[:, tsl, :] = x_ref[:, tsl, :] + gate_ref[...] * yn


def _outproj(a, x, g, gate, w, w2=None, a_time_major=False):
    B, L, D = x.shape
    K = a.shape[-1]
    glu = w2 is not None
    bt, tl = _row_blocking(B, L)
    rows = bt * tl
    row_map = lambda b, l: (b, l, 0)
    if a_time_major:
        assert bt == 1
        a = a.reshape(L, B * K)
        a_spec = pl.BlockSpec((tl, K), lambda b, l: (l, b))
    else:
        a_spec = pl.BlockSpec((bt, tl, K), row_map)
    in_specs = [a_spec,
                pl.BlockSpec((bt, tl, D), row_map),
                _resident((1, D), lambda b, l: (0, 0)),
                pl.BlockSpec((bt, 1, D), lambda b, l: (b, 0, 0)),
                _resident((K, D), lambda b, l: (0, 0))]
    args = [a, x, g, gate, w]
    vmem = 2 * rows * K * 2 + 4 * rows * D * 4 + K * D * 2 + 2 * rows * D * 4
    if glu:
        in_specs.append(_resident((K, D), lambda b, l: (0, 0)))
        args.append(w2)
        vmem += K * D * 2 + rows * D * 4
    return pl.pallas_call(
        functools.partial(_outproj_kernel, glu=glu),
        grid=(B // bt, L // tl),
        in_specs=in_specs,
        out_specs=pl.BlockSpec((bt, tl, D), row_map),
        out_shape=jax.ShapeDtypeStruct((B, L, D), F32),
        compiler_params=_params(vmem, 2),
        name="outproj_norm_residual",
    )(*args)


def _ffn_kernel(x_ref, g2_ref, sc_ref, sh_ref, g3_ref, gate_ref, wg_ref, wu_ref, wd_ref,
                o_ref, *rest):
    *ot_ref, h_scr = rest
    bt, tl, D = x_ref.shape
    f = pl.program_id(2)
    last = pl.num_programs(2) - 1

    def partial_ffn(h2):
        gg = jnp.dot(h2, wg_ref[...], preferred_element_type=F32)
        uu = jnp.dot(h2, wu_ref[...], preferred_element_type=F32)
        act = (gg * jax.nn.sigmoid(gg) * uu).astype(BF16)
        return jnp.dot(act, wd_ref[...], preferred_element_type=F32).reshape(bt, -1, D)

    @pl.when(f == 0)
    def _():
        for tsl, fr in _row_groups(bt, tl):
            h = _norm_mod(x_ref[:, tsl, :], g2_ref[...], sc_ref[...], sh_ref[...])
            h2 = h.reshape(-1, D).astype(BF16)
            h_scr[fr, :] = h2
            o_ref[:, tsl, :] = partial_ffn(h2)

    @pl.when(jnp.logical_and(f > 0, f < last))
    def _():
        o_ref[...] += partial_ffn(h_scr[...])

    @pl.when(f == last)
    def _():
        for tsl, fr in _row_groups(bt, tl):
            y = o_ref[:, tsl, :] + partial_ffn(h_scr[fr, :])
            out = x_ref[:, tsl, :] + gate_ref[...] * _rms_gain(y, g3_ref[...])
            o_ref[:, tsl, :] = out
            if ot_ref:
                ot_ref[0][fr, :] = out.reshape(-1, D)


def _ffn(x, g2, sc, sh, g3, gate, wg, wu, wd, time_major_copy=False):
    B, L, D = x.shape
    F = wg.shape[1]
    bt, tl = _row_blocking(B, L, ROW_TILE if time_major_copy else FFN_ROW_TILE)
    rows = bt * tl
    out_specs = [pl.BlockSpec((bt, tl, D), lambda b, l, f: (b, l, 0))]
    out_shape = [jax.ShapeDtypeStruct((B, L, D), F32)]
    if time_major_copy:
        assert bt == 1
        out_specs.append(pl.BlockSpec((tl, D), lambda b, l, f: (l, b)))
        out_shape.append(jax.ShapeDtypeStruct((L, B * D), F32))
    tf = FFN_COL_TILE
    assert F % tf == 0 and F // tf >= 2
    row_map = lambda b, l, f: (b, l, 0)
    mod_map = lambda b, l, f: (b, 0, 0)
    vec = lambda: _resident((1, D), lambda b, l, f: (0, 0))
    vmem = ((4 + 2 * time_major_copy) * rows * D * 4 + 3 * 2 * D * tf * 2 + rows * D * 2
            + 3 * rows * tf * 4)
    out = pl.pallas_call(
        _ffn_kernel,
        grid=(B // bt, L // tl, F // tf),
        in_specs=[pl.BlockSpec((bt, tl, D), row_map), vec(),
                  pl.BlockSpec((bt, 1, D), mod_map), pl.BlockSpec((bt, 1, D), mod_map),
                  vec(), pl.BlockSpec((bt, 1, D), mod_map),
                  pl.BlockSpec((D, tf), lambda b, l, f: (0, f)),
                  pl.BlockSpec((D, tf), lambda b, l, f: (0, f)),
                  pl.BlockSpec((tf, D), lambda b, l, f: (f, 0))],
        out_specs=out_specs,
        out_shape=out_shape,
        scratch_shapes=[pltpu.VMEM((rows, D), BF16)],
        compiler_params=_params(vmem, 3),
        name="swiglu_ffn",
    )(x, g2, sc, sh, g3, gate, wg, wu, wd)
    if time_major_copy:
        return out[0], out[1].reshape(L, B, D)
    return out[0]


def _log_sigmoid(x):
    return -(jnp.maximum(-x, 0.0) + jnp.log1p(jnp.exp(-jnp.abs(x))))


def _cumsum_rows(x):
    n = x.shape[0]
    row = lax.broadcasted_iota(jnp.int32, x.shape, 0)
    s = 1
    while s < n:
        x = x + jnp.where(row >= s, pltpu.roll(x, s, 0), 0.0)
        s *= 2
    return x


def _mlstm_kernel(q_ref, k_ref, v_ref, o_ref, gt_ref, ghn_ref, c0_ref, n0_ref, m0_ref,
                  a_ref, c_ref, n_ref, m_ref, *, valid_len):
    H, DK, DV = c_ref.shape
    Lc = q_ref.shape[0]
    scale = DK ** -0.5

    @pl.when(pl.program_id(1) == 0)
    def _():
        c_ref[...] = c0_ref[...]
        n_ref[...] = n0_ref[...]
        m_ref[...] = m0_ref[...]

    gl = gt_ref[...]
    li_all = gl
    lf_all = _log_sigmoid(gl)
    if valid_len < Lc:
        valid = lax.broadcasted_iota(jnp.int32, gl.shape, 0) < valid_len
        li_all = jnp.where(valid, li_all, -jnp.inf)
        lf_all = jnp.where(valid, lf_all, 0.0)
    b_all = _cumsum_rows(lf_all)
    causal = (lax.broadcasted_iota(jnp.int32, (Lc, Lc), 0)
              >= lax.broadcasted_iota(jnp.int32, (Lc, Lc), 1))

    for h in range(H):
        q = q_ref[:, h * DK:(h + 1) * DK]
        k = k_ref[:, h * DK:(h + 1) * DK]
        v = v_ref[:, h * DV:(h + 1) * DV]
        b = b_all[:, H + h:H + h + 1]
        g = li_all[:, h:h + 1] - b
        g_row = jnp.transpose(jnp.broadcast_to(g, (Lc, LANES)))[0:1, :]
        m_prev = m_ref[h]
        C = c_ref[h]
        n = n_ref[h]

        dmat = jnp.where(causal, b + g_row, -jnp.inf)
        inter = b + m_prev
        m_t = jnp.maximum(inter, jnp.max(dmat, axis=-1, keepdims=True))
        w = jnp.exp(dmat - m_t)
        qk = lax.dot_general(q, k, (((1,), (1,)), ((), ())), preferred_element_type=F32)
        s = qk * scale * w
        inter_w = jnp.exp(inter - m_t) * scale
        qf = q.astype(F32)
        num = (jnp.dot(s.astype(BF16), v, preferred_element_type=F32)
               + inter_w * jnp.dot(q, C.astype(BF16), preferred_element_type=F32))
        den = (jnp.sum(s, axis=-1, keepdims=True)
               + inter_w * jnp.sum(qf * n, axis=-1, keepdims=True))
        floor = jnp.maximum(jnp.abs(den), jnp.exp(-m_t))
        hh = num * (1.0 / floor)

        m_new = m_t[Lc - 1:Lc, :]
        w_last = jnp.exp(b[Lc - 1:Lc, :] + g - m_new)
        decay = jnp.exp(inter[Lc - 1:Lc, :] - m_new)
        kw = k.astype(F32) * w_last
        c_ref[h] = decay * C + lax.dot_general(kw.astype(BF16), v, (((0,), (0,)), ((), ())),
                                               preferred_element_type=F32)
        n_ref[h] = decay * n + jnp.sum(kw, axis=0, keepdims=True)
        m_ref[h] = m_new

        hn = _rms_gain(hh, ghn_ref[:, h * DV:(h + 1) * DV])
        og = o_ref[:, h * DV:(h + 1) * DV].astype(F32)
        a_ref[:, h * DV:(h + 1) * DV] = (hn * jax.nn.sigmoid(og)).astype(a_ref.dtype)


def _mlstm_core(z, gates, ghn, C0, n0, m0):
    B, L, _ = z.shape
    _, H, DK, DV = C0.shape
    HK, HV = H * DK, H * DV
    assert HV == 2 * HK and z.shape[-1] == 2 * HK + 2 * HV
    valid_len = L
    if L >= MLSTM_CHUNK:
        Lc = MLSTM_CHUNK
        assert L % Lc == 0
    else:
        Lc = LANES
        z = jnp.pad(z, ((0, 0), (0, Lc - L), (0, 0)))
        gates = jnp.pad(gates, ((0, 0), (0, Lc - L), (0, 0)))
    Lp = z.shape[1]
    blk = lambda width, idx: pl.BlockSpec((None, Lc, width), lambda b, c: (b, c, idx))
    st3 = lambda d1, d2: pl.BlockSpec((None, H, d1, d2), lambda b, c: (b, 0, 0, 0))
    vmem = (2 * Lc * (2 * HK + 2 * HV) * 2 + 2 * Lc * LANES * 4 + 2 * Lc * HV * 2
            + 4 * H * DK * DV * 4 + 12 * Lc * Lc * 4 + 8 * Lc * DV * 4 + 2 * DK * DV * 4)
    a, C, n, m = pl.pallas_call(
        functools.partial(_mlstm_kernel, valid_len=valid_len),
        grid=(B, Lp // Lc),
        in_specs=[blk(HK, 0), blk(HK, 1), blk(HV, 1), blk(HV, 2), blk(LANES, 0),
                  _resident((1, HV), lambda b, c: (0, 0)),
                  st3(DK, DV), st3(1, DK), st3(1, 1)],
        out_specs=[blk(HV, 0), st3(DK, DV), st3(1, DK), st3(1, 1)],
        out_shape=[jax.ShapeDtypeStruct((B, Lp, HV), BF16),
                   jax.ShapeDtypeStruct((B, H, DK, DV), F32),
                   jax.ShapeDtypeStruct((B, H, 1, DK), F32),
                   jax.ShapeDtypeStruct((B, H, 1, 1), F32)],
        compiler_params=_params(vmem, 2),
        name="mlstm_chunk_scan",
    )(z, z, z, z, gates, ghn, C0, n0.reshape(B, H, 1, DK), m0.reshape(B, H, 1, 1))
    return a[:, :L], C, n.reshape(B, H, DK), m.reshape(B, H)


def _conv_kernel(gb_ref, gc_ref, u_ref, w_ref, prev_ref, a_ref, st_ref):
    tl, D = gc_ref.shape
    W = w_ref.shape[0]

    @pl.when(pl.program_id(1) == 0)
    def _():
        st_ref[...] = prev_ref[...]

    z = gc_ref[...].astype(F32) * u_ref[...].astype(F32)
    row = lax.broadcasted_iota(jnp.int32, (tl, D), 0)
    conv = z * w_ref[W - 1:W, :]
    for d in range(1, W):
        zd = pltpu.roll(z, d, 0)
        for r in range(d):
            zd = jnp.where(row == r, st_ref[W - 1 - d + r:W - d + r, :], zd)
        conv = conv + zd * w_ref[W - 1 - d:W - d, :]
    a_ref[...] = (gb_ref[...].astype(F32) * conv).astype(a_ref.dtype)
    st_ref[...] = z[tl - (W - 1):, :]


def _conv_core(z3, w_conv, prev):
    B, L, D3 = z3.shape
    D = D3 // 3
    W = w_conv.shape[1]
    tl = min(L, ROW_TILE)
    assert L % tl == 0 and tl >= W - 1
    blk = lambda idx: pl.BlockSpec((None, tl, D), lambda b, l: (b, l, idx))
    st = pl.BlockSpec((None, W - 1, D), lambda b, l: (b, 0, 0))
    vmem = 2 * 4 * tl * D * 2 + 6 * tl * D * 4
    return pl.pallas_call(
        _conv_kernel,
        grid=(B, L // tl),
        in_specs=[blk(0), blk(1), blk(2), _resident((W, D), lambda b, l: (0, 0)), st],
        out_specs=[pl.BlockSpec((None, tl, D), lambda b, l: (b, l, 0)), st],
        out_shape=[jax.ShapeDtypeStruct((B, L, D), BF16),
                   jax.ShapeDtypeStruct((B, W - 1, D), prev.dtype)],
        compiler_params=_params(vmem, 2),
        name="gated_short_conv",
    )(z3, z3, z3, jnp.transpose(w_conv), prev)


def _s5_disc_kernel(ar_ref, ai_ref, ldt_ref, br_ref, bi_ref, abr_ref, abi_ref, bbr_ref, bbi_ref):
    dt = jnp.exp(ldt_ref[...])
    lr, lim = ar_ref[...], ai_ref[...]
    mag = jnp.exp(lr * dt)
    ab_re, ab_im = mag * jnp.cos(lim * dt), mag * jnp.sin(lim * dt)
    den = lr * lr + lim * lim
    nr = ab_re - 1.0
    fr = (nr * lr + ab_im * lim) / den
    fi = (ab_im * lr - nr * lim) / den
    abr_ref[...] = ab_re
    abi_ref[...] = ab_im
    br, bi = br_ref[...], bi_ref[...]
    bbr_ref[...] = fr[:, None, :] * br - fi[:, None, :] * bi
    bbi_ref[...] = fr[:, None, :] * bi + fi[:, None, :] * br


def _s5_discretize(a_re, a_im, log_dt, b_re, b_im):
    G, N, P = b_re.shape
    full = lambda *shape: pl.BlockSpec(shape, lambda: (0,) * len(shape))
    return pl.pallas_call(
        _s5_disc_kernel,
        in_specs=[full(G, N), full(G, N), full(G, 1), full(G, P, N), full(G, P, N)],
        out_specs=[full(G, N), full(G, N), full(G, P, N), full(G, P, N)],
        out_shape=[jax.ShapeDtypeStruct((G, N), F32), jax.ShapeDtypeStruct((G, N), F32),
                   jax.ShapeDtypeStruct((G, P, N), F32), jax.ShapeDtypeStruct((G, P, N), F32)],
        name="s5_discretize",
    )(a_re, a_im, log_dt.reshape(G, 1), jnp.swapaxes(b_re, 1, 2), jnp.swapaxes(b_im, 1, 2))


def _s5_kernel(x_ref, g_ref, sc_ref, sh_ref, bm_ref, cm_ref, ar_ref, ai_ref, dsk_ref,
               s0r_ref, s0i_ref, y_ref, sr_ref, si_ref, h_scr, bu_scr):
    T, B, D = x_ref.shape
    KT, KW, SW2 = bm_ref.shape
    SW = SW2 // 2
    rows = T * B
    LG = S5_LANE_GROUP * LANES

    @pl.when(pl.program_id(0) == 0)
    def _():
        sr_ref[...] = s0r_ref[...]
        si_ref[...] = s0i_ref[...]

    for t in range(T):
        h_scr[t * B:(t + 1) * B, :] = _norm_mod(x_ref[t], g_ref[...], sc_ref[...], sh_ref[...])

    for kt in range(KT):
        cols = slice(kt * KW, (kt + 1) * KW)
        bu = bu_scr.at[kt % 2]
        hk = h_scr[:, cols]
        bu[...] = jnp.dot(hk.astype(BF16), bm_ref[kt], preferred_element_type=F32)
        for lg in range(SW // LG):
            re_cols = slice(lg * LG, (lg + 1) * LG)
            im_cols = slice(SW + lg * LG, SW + (lg + 1) * LG)
            a_r = jnp.broadcast_to(ar_ref[kt, :, re_cols], (B, LG))
            a_i = jnp.broadcast_to(ai_ref[kt, :, re_cols], (B, LG))
            xr, xi = sr_ref[kt, :, re_cols], si_ref[kt, :, re_cols]
            for t in range(T):
                r = slice(t * B, (t + 1) * B)
                xr, xi = (a_r * xr - a_i * xi + bu[r, re_cols],
                          a_r * xi + a_i * xr + bu[r, im_cols])
                bu[r, re_cols] = xr
                bu[r, im_cols] = xi
            sr_ref[kt, :, re_cols] = xr
            si_ref[kt, :, re_cols] = xi
        yk = jnp.dot(bu[...].astype(BF16), cm_ref[kt], preferred_element_type=F32)
        yk = yk + dsk_ref[:, cols] * hk
        y_ref[:, :, cols] = jax.nn.gelu(yk).reshape(T, B, KW).astype(y_ref.dtype)


def _s5_core(xt, g, sc, sh, bmat, cmat, a_r, a_i, dsk, s0r, s0i):
    L, B, D = xt.shape
    KT, KW, SW2 = bmat.shape
    SW = SW2 // 2
    assert B == SUBLANES and SW % (S5_LANE_GROUP * LANES) == 0
    T = min(S5_CHUNK, L)
    assert L % T == 0
    rows = T * B
    c0 = lambda *shape: _resident(shape, lambda c: (0,) * len(shape))
    vmem = (2 * rows * D * 4 + 2 * rows * D * 2 + 2 * KT * KW * SW2 * 2 + rows * D * 4
            + 3 * rows * SW2 * 4 + 6 * KT * B * SW * 4)
    return pl.pallas_call(
        _s5_kernel,
        grid=(L // T,),
        in_specs=[pl.BlockSpec((T, B, D), lambda c: (c, 0, 0)), c0(1, D), c0(B, D), c0(B, D),
                  c0(KT, KW, SW2), c0(KT, SW2, KW), c0(KT, 1, SW), c0(KT, 1, SW), c0(1, D),
                  c0(KT, B, SW), c0(KT, B, SW)],
        out_specs=[pl.BlockSpec((T, B, D), lambda c: (c, 0, 0)),
                   pl.BlockSpec((KT, B, SW), lambda c: (0, 0, 0)),
                   pl.BlockSpec((KT, B, SW), lambda c: (0, 0, 0))],
        out_shape=[jax.ShapeDtypeStruct((L, B, D), BF16),
                   jax.ShapeDtypeStruct((KT, B, SW), F32),
                   jax.ShapeDtypeStruct((KT, B, SW), F32)],
        scratch_shapes=[pltpu.VMEM((rows, D), F32), pltpu.VMEM((2, rows, SW2), F32)],
        compiler_params=_params(vmem, 1),
        name="s5_scan",
    )(xt, g, sc, sh, bmat, cmat, a_r, a_i, dsk, s0r, s0i)


def _s5_weights(a_re, a_im, b_re, b_im, c_re, c_im, d_skip, log_dt):
    G, N, P = b_re.shape
    KW = MXU_DIM_V7X
    GP = KW // P
    KT = G // GP
    ab_re, ab_im, bb_re, bb_im = _s5_discretize(a_re, a_im, log_dt, b_re, b_im)
    eye = jnp.eye(GP, dtype=F32)
    bb = jnp.stack([bb_re, bb_im]).reshape(2, KT, GP, P, N)
    bmat = jnp.einsum('rkgpn,gh->kgprhn', bb, eye).reshape(KT, KW, 2 * GP * N)
    cc = jnp.stack([c_re, -c_im]).reshape(2, KT, GP, P, N)
    cmat = jnp.einsum('rkgpn,gh->krgnhp', cc, eye).reshape(KT, 2 * GP * N, KW)
    a_r = ab_re.reshape(KT, 1, GP * N)
    a_i = ab_im.reshape(KT, 1, GP * N)
    return bmat.astype(BF16), cmat.astype(BF16), a_r, a_i, d_skip.reshape(1, G * P)


def _trunk(x, mod, row0, st_C, st_n, st_m, st_conv, st_re, st_im, p):
    B, L, D = x.shape
    depth = p['g_norm'].shape[0]
    new_C, new_n, new_m, new_conv, new_re, new_im = [], [], [], [], [], []
    tiled = L >= ROW_TILE
    xt = None
    for i in range(depth):
        m6 = mod[i, row0:row0 + B].reshape(B, 6, 1, D)
        sh1, sc1, g1, sh2, sc2, g2 = (m6[:, j] for j in range(6))
        gn = p['g_norm'][i]
        gvec = lambda r: gn[r].reshape(1, D)
        kind, j = i % N_MIXERS, i // N_MIXERS
        if kind == 0:
            w = p['wA_in'][j]
            nz = w.shape[1] - 2 * st_C.shape[2]
            H = st_C.shape[2]
            w_gates = jnp.pad(w[:, nz:], ((0, 0), (0, LANES - 2 * H))).astype(BF16)
            b_gates = jnp.pad(p['bA_gates'][j], (0, LANES - 2 * H)).reshape(1, LANES)
            z, gates = _inproj(x, gvec(0), sc1, sh1, w[:, :nz].astype(BF16), w_gates, b_gates)
            a, C, n, m = _mlstm_core(z, gates, p['gA_hnorm'][j].reshape(1, -1),
                                     st_C[j], st_n[j], st_m[j])
            new_C.append(C); new_n.append(n); new_m.append(m)
            x = _outproj(a, x, gvec(1), g1, p['wA_out'][j].astype(BF16))
        elif kind == 1:
            z3 = _inproj(x, gvec(0), sc1, sh1, p['wB_in'][j].astype(BF16))
            a, cv = _conv_core(z3, p['wB_conv'][j], st_conv[j])
            new_conv.append(cv)
            x = _outproj(a, x, gvec(1), g1, p['wB_out'][j].astype(BF16))
        else:
            bmat, cmat, a_r, a_i, dsk = _s5_weights(
                p['s5_A_re'][j], p['s5_A_im'][j], p['s5_B_re'][j], p['s5_B_im'][j],
                p['s5_C_re'][j], p['s5_C_im'][j], p['s5_D'][j], p['s5_log_dt'][j])
            KT, _, SW2 = bmat.shape
            to_lanes = lambda s: jnp.swapaxes(s.reshape(B, KT, SW2 // 2), 0, 1)
            if xt is None:
                xt = jnp.swapaxes(x, 0, 1)
            yt, sr, si = _s5_core(xt, gvec(0), sc1.reshape(B, D),
                                  sh1.reshape(B, D), bmat, cmat, a_r, a_i, dsk,
                                  to_lanes(st_re[j]), to_lanes(st_im[j]))
            from_lanes = lambda s: jnp.swapaxes(s, 0, 1).reshape(st_re[j].shape)
            new_re.append(from_lanes(sr)); new_im.append(from_lanes(si))
            wc = p['wC_out'][j]
            x = _outproj(yt if tiled else jnp.swapaxes(yt, 0, 1), x, gvec(1), g1,
                         wc[:, :D].astype(BF16), wc[:, D:].astype(BF16), a_time_major=tiled)
        want_xt = tiled and i + 1 < depth and (i + 1) % N_MIXERS == 2
        x = _ffn(x, gvec(2), sc2, sh2, gvec(3), g2, p['w_ffn_gate'][i].astype(BF16),
                 p['w_ffn_up'][i].astype(BF16), p['w_ffn_down'][i].astype(BF16),
                 time_major_copy=want_xt)
        xt = None
        if want_xt:
            x, xt = x
    return (x, jnp.stack(new_C), jnp.stack(new_n), jnp.stack(new_m), jnp.stack(new_conv),
            jnp.stack(new_re), jnp.stack(new_im))


def kernel(x_prompt, x_sample, state_mlstm_C, state_mlstm_n, state_mlstm_m, state_conv,
           state_s5_re, state_s5_im, c_prompt, c_sample, w_mod, b_mod, g_norm, wA_in,
           bA_gates, gA_hnorm, wA_out, wB_in, wB_conv, wB_out, s5_A_re, s5_A_im, s5_B_re,
           s5_B_im, s5_C_re, s5_C_im, s5_D, s5_log_dt, wC_out, w_ffn_gate, w_ffn_up,
           w_ffn_down):
    p = dict(g_norm=g_norm, wA_in=wA_in, bA_gates=bA_gates, gA_hnorm=gA_hnorm, wA_out=wA_out,
             wB_in=wB_in, wB_conv=wB_conv, wB_out=wB_out, s5_A_re=s5_A_re, s5_A_im=s5_A_im,
             s5_B_re=s5_B_re, s5_B_im=s5_B_im, s5_C_re=s5_C_re, s5_C_im=s5_C_im, s5_D=s5_D,
             s5_log_dt=s5_log_dt, wC_out=wC_out, w_ffn_gate=w_ffn_gate, w_ffn_up=w_ffn_up,
             w_ffn_down=w_ffn_down)
    bp = x_prompt.shape[0]
    mod = _modulation(jnp.concatenate([c_prompt, c_sample], axis=0), w_mod, b_mod)
    zeros = lambda s: jnp.zeros((s.shape[0], bp) + s.shape[2:], s.dtype)
    outs_p = _trunk(x_prompt, mod, 0, zeros(state_mlstm_C), zeros(state_mlstm_n),
                    zeros(state_mlstm_m), zeros(state_conv), zeros(state_s5_re),
                    zeros(state_s5_im), p)
    outs_s = _trunk(x_sample, mod, bp, state_mlstm_C, state_mlstm_n, state_mlstm_m,
                    state_conv, state_s5_re, state_s5_im, p)
    return (outs_p[0], outs_s[0]) + tuple(outs_p[1:]) + tuple(outs_s[1:])
```

```python
import functools

import jax
import jax.numpy as jnp
from jax import lax
from jax.experimental import pallas as pl
from jax.experimental.pallas import tpu as pltpu

F32 = jnp.float32
BF16 = jnp.bfloat16
EPS = 1e-6
N_MIXERS = 3

LANES = 128
SUBLANES = 8
MXU_DIM_V7X = 256
VMEM_BYTES_V7X = 64 * 1024 * 1024
VMEM_CAP = VMEM_BYTES_V7X - 6 * 1024 * 1024

ROW_TILE = 512
ROW_GROUP = MXU_DIM_V7X
FFN_ROW_TILE = 1024
INPROJ_ROW_TILE = 1024
INPROJ_COL_TILE = 2048
FFN_COL_TILE = 512
MOD_COL_TILE = 1024
MLSTM_CHUNK = 256
S5_CHUNK = 32
S5_LANE_GROUP = 4


def _params(vmem_bytes, n_grid):
    limit = int(min(VMEM_CAP, max(vmem_bytes * 5 // 4 + (4 << 20), 16 << 20)))
    return pltpu.CompilerParams(dimension_semantics=("arbitrary",) * n_grid,
                                vmem_limit_bytes=limit)


def _resident(block_shape, index_map):
    return pl.BlockSpec(block_shape, index_map, pipeline_mode=pl.Buffered(1))


def _row_blocking(B, L, tile=ROW_TILE):
    if L >= ROW_TILE:
        tile = min(tile, L)
        assert L % tile == 0
        return 1, tile
    assert L % SUBLANES == 0
    return B, L


def _norm_mod(x, g, sc, sh):
    ms = jnp.mean(x * x, axis=-1, keepdims=True)
    y = x * lax.rsqrt(ms + EPS) * g
    return y * (1.0 + sc) + sh


def _rms_gain(y, g):
    ms = jnp.mean(y * y, axis=-1, keepdims=True)
    return y * lax.rsqrt(ms + EPS) * g


def _row_groups(bt, tl):
    if bt != 1 or tl <= ROW_GROUP:
        return [(slice(0, tl), slice(0, bt * tl))]
    assert tl % ROW_GROUP == 0
    return [(slice(q * ROW_GROUP, (q + 1) * ROW_GROUP),) * 2 for q in range(tl // ROW_GROUP)]


def _mod_kernel(c_ref, w_ref, b_ref, o_ref):
    c = c_ref[...]
    sc = (c * jax.nn.sigmoid(c)).astype(BF16)
    o_ref[...] = jnp.dot(sc, w_ref[...].astype(BF16), preferred_element_type=F32) + b_ref[...]


def _modulation(c_all, w_mod, b_mod):
    depth, D, N = w_mod.shape
    R = c_all.shape[0]
    tn = min(MOD_COL_TILE, N)
    assert N % tn == 0
    vmem = 2 * D * tn * 4 + 2 * R * tn * 4 + R * D * 4
    return pl.pallas_call(
        _mod_kernel,
        grid=(depth, N // tn),
        in_specs=[_resident((R, D), lambda i, j: (0, 0)),
                  pl.BlockSpec((None, D, tn), lambda i, j: (i, 0, j)),
                  pl.BlockSpec((None, 1, tn), lambda i, j: (i, 0, j))],
        out_specs=pl.BlockSpec((None, R, tn), lambda i, j: (i, 0, j)),
        out_shape=jax.ShapeDtypeStruct((depth, R, N), F32),
        compiler_params=_params(vmem, 2),
        name="adaln_modulation",
    )(c_all, w_mod, b_mod.reshape(depth, 1, N))


def _inproj_kernel(x_ref, g_ref, sc_ref, sh_ref, w_ref, *rest, with_gates):
    if with_gates:
        wg_ref, bg_ref, z_ref, gates_ref, h_scr = rest
    else:
        z_ref, h_scr = rest
    bt, tl, D = x_ref.shape
    j = pl.program_id(2)

    @pl.when(j == 0)
    def _():
        for tsl, fr in _row_groups(bt, tl):
            h = _norm_mod(x_ref[:, tsl, :], g_ref[...], sc_ref[...], sh_ref[...])
            h2 = h.reshape(-1, D).astype(BF16)
            h_scr[fr, :] = h2
            z = jnp.dot(h2, w_ref[...], preferred_element_type=F32)
            z_ref[:, tsl, :] = z.reshape(bt, -1, z.shape[-1]).astype(z_ref.dtype)
            if with_gates:
                gates = jnp.dot(h2, wg_ref[...], preferred_element_type=F32) + bg_ref[...]
                gates_ref[:, tsl, :] = gates.reshape(bt, -1, gates.shape[-1])

    @pl.when(j > 0)
    def _():
        z = jnp.dot(h_scr[...], w_ref[...], preferred_element_type=F32)
        z_ref[...] = z.reshape(z_ref.shape).astype(z_ref.dtype)


def _inproj(x, g, sc, sh, w, layer, N, w_gates=None, b_gates=None):
    B, L, D = x.shape
    bt, tl = _row_blocking(B, L, INPROJ_ROW_TILE)
    rows = bt * tl
    tn = INPROJ_COL_TILE
    while N % tn:
        tn //= 2
    assert tn % LANES == 0
    with_gates = w_gates is not None
    row_map = lambda b, l, j: (b, l, 0)
    mod_map = lambda b, l, j: (b, 0, 0)
    in_specs = [pl.BlockSpec((bt, tl, D), row_map),
                _resident((1, D), lambda b, l, j: (0, 0)),
                pl.BlockSpec((bt, 1, D), mod_map),
                pl.BlockSpec((bt, 1, D), mod_map),
                pl.BlockSpec((None, D, tn), lambda b, l, j: (layer, 0, j))]
    args = [x, g, sc, sh, w]
    out_specs = [pl.BlockSpec((bt, tl, tn), lambda b, l, j: (b, l, j))]
    out_shape = [jax.ShapeDtypeStruct((B, L, N), BF16)]
    vmem = 2 * rows * D * 4 + 2 * D * tn * 2 + 2 * rows * tn * 2 + rows * D * 2
    if with_gates:
        in_specs += [_resident((D, LANES), lambda b, l, j: (0, 0)),
                     _resident((1, LANES), lambda b, l, j: (0, 0))]
        args += [w_gates, b_gates]
        out_specs.append(pl.BlockSpec((bt, tl, LANES), row_map))
        out_shape.append(jax.ShapeDtypeStruct((B, L, LANES), F32))
        vmem += D * LANES * 2 + 2 * rows * LANES * 4
    out = pl.pallas_call(
        functools.partial(_inproj_kernel, with_gates=with_gates),
        grid=(B // bt, L // tl, N // tn),
        in_specs=in_specs,
        out_specs=out_specs,
        out_shape=out_shape,
        scratch_shapes=[pltpu.VMEM((rows, D), BF16)],
        compiler_params=_params(vmem, 3),
        name="norm_mod_inproj",
    )(*args)
    return out if with_gates else out[0]


def _outproj_kernel(a_ref, x_ref, g_ref, gate_ref, w_ref, *rest, glu):
    if glu:
        w2_ref, o_ref = rest
    else:
        (o_ref,) = rest
    bt, tl, D = x_ref.shape
    for tsl, fr in _row_groups(bt, tl):
        a = a_ref[fr, :] if len(a_ref.shape) == 2 else a_ref[:, tsl, :].reshape(-1, a_ref.shape[-1])
        y = jnp.dot(a, w_ref[...], preferred_element_type=F32)
        if glu:
            y = y * jax.nn.sigmoid(jnp.dot(a, w2_ref[...], preferred_element_type=F32))
        yn = _rms_gain(y, g_ref[...]).reshape(bt, -1, D)
        o_ref[:, tsl, :] = x_ref[:, tsl, :] + gate_ref[...] * yn


def _outproj(a, x, g, gate, w, layer, glu=False, a_time_major=False):
    B, L, D = x.shape
    K = a.shape[-1]
    bt, tl = _row_blocking(B, L)
    rows = bt * tl
    row_map = lambda b, l: (b, l, 0)
    if a_time_major:
        assert bt == 1
        a = a.reshape(L, B * K)
        a_spec = pl.BlockSpec((tl, K), lambda b, l: (l, b))
    else:
        a_spec = pl.BlockSpec((bt, tl, K), row_map)
    in_specs = [a_spec,
                pl.BlockSpec((bt, tl, D), row_map),
                _resident((1, D), lambda b, l: (0, 0)),
                pl.BlockSpec((bt, 1, D), lambda b, l: (b, 0, 0)),
                _resident((None, K, D), lambda b, l: (layer, 0, 0))]
    args = [a, x, g, gate, w]
    vmem = 2 * rows * K * 2 + 4 * rows * D * 4 + K * D * 2 + 2 * rows * D * 4
    if glu:
        in_specs.append(_resident((None, K, D), lambda b, l: (layer, 0, 1)))
        args.append(w)
        vmem += K * D * 2 + rows * D * 4
    return pl.pallas_call(
        functools.partial(_outproj_kernel, glu=glu),
        grid=(B // bt, L // tl),
        in_specs=in_specs,
        out_specs=pl.BlockSpec((bt, tl, D), row_map),
        out_shape=jax.ShapeDtypeStruct((B, L, D), F32),
        compiler_params=_params(vmem, 2),
        name="outproj_norm_residual",
    )(*args)


def _ffn_kernel(x_ref, g2_ref, sc_ref, sh_ref, g3_ref, gate_ref, wg_ref, wu_ref, wd_ref,
                o_ref, *rest):
    *ot_ref, h_scr = rest
    bt, tl, D = x_ref.shape
    f = pl.program_id(2)
    last = pl.num_programs(2) - 1

    def partial_ffn(h2):
        gg = jnp.dot(h2, wg_ref[...], preferred_element_type=F32)
        uu = jnp.dot(h2, wu_ref[...], preferred_element_type=F32)
        act = (gg * jax.nn.sigmoid(gg) * uu).astype(BF16)
        return jnp.dot(act, wd_ref[...], preferred_element_type=F32).reshape(bt, -1, D)

    @pl.when(f == 0)
    def _():
        for tsl, fr in _row_groups(bt, tl):
            h = _norm_mod(x_ref[:, tsl, :], g2_ref[...], sc_ref[...], sh_ref[...])
            h2 = h.reshape(-1, D).astype(BF16)
            h_scr[fr, :] = h2
            o_ref[:, tsl, :] = partial_ffn(h2)

    @pl.when(jnp.logical_and(f > 0, f < last))
    def _():
        o_ref[...] += partial_ffn(h_scr[...])

    @pl.when(f == last)
    def _():
        for tsl, fr in _row_groups(bt, tl):
            y = o_ref[:, tsl, :] + partial_ffn(h_scr[fr, :])
            out = x_ref[:, tsl, :] + gate_ref[...] * _rms_gain(y, g3_ref[...])
            o_ref[:, tsl, :] = out
            if ot_ref:
                ot_ref[0][fr, :] = out.reshape(-1, D)


def _ffn(x, g2, sc, sh, g3, gate, wg, wu, wd, layer, time_major_copy=False):
    B, L, D = x.shape
    F = wg.shape[-1]
    bt, tl = _row_blocking(B, L, ROW_TILE if time_major_copy else FFN_ROW_TILE)
    rows = bt * tl
    out_specs = [pl.BlockSpec((bt, tl, D), lambda b, l, f: (b, l, 0))]
    out_shape = [jax.ShapeDtypeStruct((B, L, D), F32)]
    if time_major_copy:
        assert bt == 1
        out_specs.append(pl.BlockSpec((tl, D), lambda b, l, f: (l, b)))
        out_shape.append(jax.ShapeDtypeStruct((L, B * D), F32))
    tf = FFN_COL_TILE
    assert F % tf == 0 and F // tf >= 2
    row_map = lambda b, l, f: (b, l, 0)
    mod_map = lambda b, l, f: (b, 0, 0)
    vec = lambda: _resident((1, D), lambda b, l, f: (0, 0))
    vmem = ((4 + 2 * time_major_copy) * rows * D * 4 + 3 * 2 * D * tf * 2 + rows * D * 2
            + 3 * rows * tf * 4)
    out = pl.pallas_call(
        _ffn_kernel,
        grid=(B // bt, L // tl, F // tf),
        in_specs=[pl.BlockSpec((bt, tl, D), row_map), vec(),
                  pl.BlockSpec((bt, 1, D), mod_map), pl.BlockSpec((bt, 1, D), mod_map),
                  vec(), pl.BlockSpec((bt, 1, D), mod_map),
                  pl.BlockSpec((None, D, tf), lambda b, l, f: (layer, 0, f)),
                  pl.BlockSpec((None, D, tf), lambda b, l, f: (layer, 0, f)),
                  pl.BlockSpec((None, tf, D), lambda b, l, f: (layer, f, 0))],
        out_specs=out_specs,
        out_shape=out_shape,
        scratch_shapes=[pltpu.VMEM((rows, D), BF16)],
        compiler_params=_params(vmem, 3),
        name="swiglu_ffn",
    )(x, g2, sc, sh, g3, gate, wg, wu, wd)
    if time_major_copy:
        return out[0], out[1].reshape(L, B, D)
    return out[0]


def _log_sigmoid(x):
    return -(jnp.maximum(-x, 0.0) + jnp.log1p(jnp.exp(-jnp.abs(x))))


def _cumsum_rows(x):
    n = x.shape[0]
    row = lax.broadcasted_iota(jnp.int32, x.shape, 0)
    s = 1
    while s < n:
        x = x + jnp.where(row >= s, pltpu.roll(x, s, 0), 0.0)
        s *= 2
    return x


def _mlstm_kernel(q_ref, k_ref, v_ref, o_ref, gt_ref, ghn_ref, c0_ref, n0_ref, m0_ref,
                  x_ref, g1_ref, gate_ref, wout_ref, xo_ref, c_ref, n_ref, m_ref, *, valid_len):
    H, DK, DV = c_ref.shape
    Lc = q_ref.shape[0]
    scale = DK ** -0.5

    @pl.when(pl.program_id(1) == 0)
    def _():
        c_ref[...] = c0_ref[...]
        n_ref[...] = n0_ref[...]
        m_ref[...] = m0_ref[...]

    gl = gt_ref[...]
    li_all = gl
    lf_all = _log_sigmoid(gl)
    if valid_len < Lc:
        valid = lax.broadcasted_iota(jnp.int32, gl.shape, 0) < valid_len
        li_all = jnp.where(valid, li_all, -jnp.inf)
        lf_all = jnp.where(valid, lf_all, 0.0)
    b_all = _cumsum_rows(lf_all)
    causal = (lax.broadcasted_iota(jnp.int32, (Lc, Lc), 0)
              >= lax.broadcasted_iota(jnp.int32, (Lc, Lc), 1))

    heads = range(H)
    q = [q_ref[:, h * DK:(h + 1) * DK] for h in heads]
    k = [k_ref[:, h * DK:(h + 1) * DK] for h in heads]
    v = [v_ref[:, h * DV:(h + 1) * DV] for h in heads]
    C = [c_ref[h] for h in heads]
    n = [n_ref[h] for h in heads]

    qk = [lax.dot_general(q[h], k[h], (((1,), (1,)), ((), ())), preferred_element_type=F32)
          for h in heads]
    qC = [jnp.dot(q[h], C[h].astype(BF16), preferred_element_type=F32) for h in heads]
    b, g, inter, m_t, w = [], [], [], [], []
    for h in heads:
        b.append(b_all[:, H + h:H + h + 1])
        g.append(li_all[:, h:h + 1] - b[h])
        g_row = jnp.transpose(jnp.broadcast_to(g[h], (Lc, LANES)))[0:1, :]
        dmat = jnp.where(causal, b[h] + g_row, -jnp.inf)
        inter.append(b[h] + m_ref[h])
        m_t.append(jnp.maximum(inter[h], jnp.max(dmat, axis=-1, keepdims=True)))
        w.append(jnp.exp(dmat - m_t[h]))

    sv, den_s, kw = [], [], []
    for h in heads:
        s = qk[h] * scale * w[h]
        den_s.append(jnp.sum(s, axis=-1, keepdims=True))
        sv.append(jnp.dot(s.astype(BF16), v[h], preferred_element_type=F32))
        m_new = m_t[h][Lc - 1:Lc, :]
        w_last = jnp.exp(b[h][Lc - 1:Lc, :] + g[h] - m_new)
        decay = jnp.exp(inter[h][Lc - 1:Lc, :] - m_new)
        kw.append(k[h].astype(F32) * w_last)
        c_ref[h] = decay * C[h] + lax.dot_general(
            kw[h].astype(BF16), v[h], (((0,), (0,)), ((), ())), preferred_element_type=F32)
        n_ref[h] = decay * n[h] + jnp.sum(kw[h], axis=0, keepdims=True)
        m_ref[h] = m_new

    y = None
    for h in heads:
        inter_w = jnp.exp(inter[h] - m_t[h]) * scale
        num = sv[h] + inter_w * qC[h]
        den = den_s[h] + inter_w * jnp.sum(q[h].astype(F32) * n[h], axis=-1, keepdims=True)
        floor = jnp.maximum(jnp.abs(den), jnp.exp(-m_t[h]))
        hh = num * (1.0 / floor)
        hn = _rms_gain(hh, ghn_ref[:, h * DV:(h + 1) * DV])
        og = o_ref[:, h * DV:(h + 1) * DV].astype(F32)
        a = (hn * jax.nn.sigmoid(og)).astype(BF16)
        yh = jnp.dot(a, wout_ref[h * DV:(h + 1) * DV, :], preferred_element_type=F32)
        y = yh if y is None else y + yh


    xo_ref[...] = x_ref[...] + gate_ref[...] * _rms_gain(y, g1_ref[...])


def _mlstm_mixer(z, gates, ghn, C0, layer, n0, m0, x, g1, gate, wout):
    B, L, D = x.shape
    _, _, H, DK, DV = C0.shape
    HK, HV = H * DK, H * DV
    assert HV == 2 * HK and z.shape[-1] == 2 * HK + 2 * HV
    valid_len = L
    if L >= MLSTM_CHUNK:
        Lc = MLSTM_CHUNK
        assert L % Lc == 0
    else:
        Lc = LANES
        pad = ((0, 0), (0, Lc - L), (0, 0))
        z, gates, x = jnp.pad(z, pad), jnp.pad(gates, pad), jnp.pad(x, pad)
    Lp = z.shape[1]
    blk = lambda width, idx: pl.BlockSpec((None, Lc, width), lambda b, c: (b, c, idx))
    st3 = lambda d1, d2: pl.BlockSpec((None, H, d1, d2), lambda b, c: (b, 0, 0, 0))
    vmem = (2 * Lc * (2 * HK + 2 * HV) * 2 + 2 * Lc * LANES * 4 + 4 * Lc * D * 4
            + 4 * H * DK * DV * 4 + 12 * Lc * Lc * 4 + 8 * Lc * DV * 4 + 2 * DK * DV * 4
            + HV * D * 2 + 3 * Lc * D * 4)
    xo, C, n, m = pl.pallas_call(
        functools.partial(_mlstm_kernel, valid_len=valid_len),
        grid=(B, Lp // Lc),
        in_specs=[blk(HK, 0), blk(HK, 1), blk(HV, 1), blk(HV, 2), blk(LANES, 0),
                  _resident((1, HV), lambda b, c: (0, 0)),
                  pl.BlockSpec((None, None, H, DK, DV), lambda b, c: (layer, b, 0, 0, 0)),
                  st3(1, DK), st3(1, 1),
                  blk(D, 0), _resident((1, D), lambda b, c: (0, 0)),
                  pl.BlockSpec((None, 1, D), lambda b, c: (b, 0, 0)),
                  _resident((None, HV, D), lambda b, c: (layer, 0, 0))],
        out_specs=[blk(D, 0), st3(DK, DV), st3(1, DK), st3(1, 1)],
        out_shape=[jax.ShapeDtypeStruct((B, Lp, D), F32),
                   jax.ShapeDtypeStruct((B, H, DK, DV), F32),
                   jax.ShapeDtypeStruct((B, H, 1, DK), F32),
                   jax.ShapeDtypeStruct((B, H, 1, 1), F32)],
        compiler_params=_params(vmem, 2),
        name="mlstm_mixer",
    )(z, z, z, z, gates, ghn, C0, n0.reshape(B, H, 1, DK), m0.reshape(B, H, 1, 1),
      x, g1, gate, wout)
    return xo[:, :L], C, n.reshape(B, H, DK), m.reshape(B, H)


def _conv_kernel(gb_ref, gc_ref, u_ref, w_ref, prev_ref, x_ref, g1_ref, gate_ref, wout_ref,
                 xo_ref, st_ref):
    tl, D = gc_ref.shape
    W = w_ref.shape[0]

    @pl.when(pl.program_id(1) == 0)
    def _():
        st_ref[...] = prev_ref[...]

    z = gc_ref[...].astype(F32) * u_ref[...].astype(F32)
    row = lax.broadcasted_iota(jnp.int32, (tl, D), 0)
    conv = z * w_ref[W - 1:W, :]
    for d in range(1, W):
        zd = pltpu.roll(z, d, 0)
        for r in range(d):
            zd = jnp.where(row == r, st_ref[W - 1 - d + r:W - d + r, :], zd)
        conv = conv + zd * w_ref[W - 1 - d:W - d, :]
    a = (gb_ref[...].astype(F32) * conv).astype(BF16)
    st_ref[...] = z[tl - (W - 1):, :]
    y = jnp.dot(a, wout_ref[...], preferred_element_type=F32)
    xo_ref[...] = x_ref[...] + gate_ref[...] * _rms_gain(y, g1_ref[...])


def _conv_mixer(z3, w_conv, prev, x, g1, gate, wout, layer):
    B, L, D = x.shape
    W = w_conv.shape[1]
    tl = min(L, ROW_TILE)
    assert L % tl == 0 and tl >= W - 1 and z3.shape[-1] == 3 * D
    blk = lambda idx: pl.BlockSpec((None, tl, D), lambda b, l: (b, l, idx))
    st = pl.BlockSpec((None, W - 1, D), lambda b, l: (b, 0, 0))
    vmem = 2 * 3 * tl * D * 2 + 4 * tl * D * 4 + D * D * 2 + 6 * tl * D * 4
    return pl.pallas_call(
        _conv_kernel,
        grid=(B, L // tl),
        in_specs=[blk(0), blk(1), blk(2), _resident((W, D), lambda b, l: (0, 0)), st,
                  blk(0), _resident((1, D), lambda b, l: (0, 0)),
                  pl.BlockSpec((None, 1, D), lambda b, l: (b, 0, 0)),
                  _resident((None, D, D), lambda b, l: (layer, 0, 0))],
        out_specs=[blk(0), st],
        out_shape=[jax.ShapeDtypeStruct((B, L, D), F32),
                   jax.ShapeDtypeStruct((B, W - 1, D), prev.dtype)],
        compiler_params=_params(vmem, 2),
        name="conv_mixer",
    )(z3, z3, z3, jnp.transpose(w_conv), prev, x, g1, gate, wout)


def _s5_disc_kernel(ar_ref, ai_ref, ldt_ref, br_ref, bi_ref, abr_ref, abi_ref, bbr_ref, bbi_ref):
    dt = jnp.exp(ldt_ref[...])
    lr, lim = ar_ref[...], ai_ref[...]
    mag = jnp.exp(lr * dt)
    ab_re, ab_im = mag * jnp.cos(lim * dt), mag * jnp.sin(lim * dt)
    den = lr * lr + lim * lim
    nr = ab_re - 1.0
    fr = (nr * lr + ab_im * lim) / den
    fi = (ab_im * lr - nr * lim) / den
    abr_ref[...] = ab_re
    abi_ref[...] = ab_im
    br, bi = br_ref[...], bi_ref[...]
    bbr_ref[...] = fr[:, None, :] * br - fi[:, None, :] * bi
    bbi_ref[...] = fr[:, None, :] * bi + fi[:, None, :] * br


def _s5_discretize(a_re, a_im, log_dt, b_re, b_im):
    G, N, P = b_re.shape
    full = lambda *shape: pl.BlockSpec(shape, lambda: (0,) * len(shape))
    return pl.pallas_call(
        _s5_disc_kernel,
        in_specs=[full(G, N), full(G, N), full(G, 1), full(G, P, N), full(G, P, N)],
        out_specs=[full(G, N), full(G, N), full(G, P, N), full(G, P, N)],
        out_shape=[jax.ShapeDtypeStruct((G, N), F32), jax.ShapeDtypeStruct((G, N), F32),
                   jax.ShapeDtypeStruct((G, P, N), F32), jax.ShapeDtypeStruct((G, P, N), F32)],
        name="s5_discretize",
    )(a_re, a_im, log_dt.reshape(G, 1), jnp.swapaxes(b_re, 1, 2), jnp.swapaxes(b_im, 1, 2))


def _s5_kernel(x_ref, g_ref, sc_ref, sh_ref, bm_ref, cm_ref, ar_ref, ai_ref, dsk_ref,
               s0r_ref, s0i_ref, y_ref, sr_ref, si_ref, h_scr, bu_scr):
    T, B, D = x_ref.shape
    KT, KW, SW2 = bm_ref.shape
    SW = SW2 // 2
    rows = T * B
    LG = S5_LANE_GROUP * LANES

    @pl.when(pl.program_id(0) == 0)
    def _():
        sr_ref[...] = s0r_ref[...]
        si_ref[...] = s0i_ref[...]

    for t in range(T):
        h_scr[t * B:(t + 1) * B, :] = _norm_mod(x_ref[t], g_ref[...], sc_ref[...], sh_ref[...])

    for kt in range(KT):
        cols = slice(kt * KW, (kt + 1) * KW)
        bu = bu_scr.at[kt % 2]
        hk = h_scr[:, cols]
        bu[...] = jnp.dot(hk.astype(BF16), bm_ref[kt], preferred_element_type=F32)
        for lg in range(SW // LG):
            re_cols = slice(lg * LG, (lg + 1) * LG)
            im_cols = slice(SW + lg * LG, SW + (lg + 1) * LG)
            a_r = jnp.broadcast_to(ar_ref[kt, :, re_cols], (B, LG))
            a_i = jnp.broadcast_to(ai_ref[kt, :, re_cols], (B, LG))
            xr, xi = sr_ref[kt, :, re_cols], si_ref[kt, :, re_cols]
            for t in range(T):
                r = slice(t * B, (t + 1) * B)
                xr, xi = (a_r * xr - a_i * xi + bu[r, re_cols],
                          a_r * xi + a_i * xr + bu[r, im_cols])
                bu[r, re_cols] = xr
                bu[r, im_cols] = xi
            sr_ref[kt, :, re_cols] = xr
            si_ref[kt, :, re_cols] = xi
        yk = jnp.dot(bu[...].astype(BF16), cm_ref[kt], preferred_element_type=F32)
        yk = yk + dsk_ref[:, cols] * hk
        y_ref[:, :, cols] = jax.nn.gelu(yk).reshape(T, B, KW).astype(y_ref.dtype)


def _s5_core(xt, g, sc, sh, bmat, cmat, a_r, a_i, dsk, s0r, s0i):
    L, B, D = xt.shape
    KT, KW, SW2 = bmat.shape
    SW = SW2 // 2
    assert B == SUBLANES and SW % (S5_LANE_GROUP * LANES) == 0
    T = min(S5_CHUNK, L)
    assert L % T == 0
    rows = T * B
    c0 = lambda *shape: _resident(shape, lambda c: (0,) * len(shape))
    vmem = (2 * rows * D * 4 + 2 * rows * D * 2 + 2 * KT * KW * SW2 * 2 + rows * D * 4
            + 3 * rows * SW2 * 4 + 6 * KT * B * SW * 4)
    return pl.pallas_call(
        _s5_kernel,
        grid=(L // T,),
        in_specs=[pl.BlockSpec((T, B, D), lambda c: (c, 0, 0)), c0(1, D), c0(B, D), c0(B, D),
                  c0(KT, KW, SW2), c0(KT, SW2, KW), c0(KT, 1, SW), c0(KT, 1, SW), c0(1, D),
                  c0(KT, B, SW), c0(KT, B, SW)],
        out_specs=[pl.BlockSpec((T, B, D), lambda c: (c, 0, 0)),
                   pl.BlockSpec((KT, B, SW), lambda c: (0, 0, 0)),
                   pl.BlockSpec((KT, B, SW), lambda c: (0, 0, 0))],
        out_shape=[jax.ShapeDtypeStruct((L, B, D), BF16),
                   jax.ShapeDtypeStruct((KT, B, SW), F32),
                   jax.ShapeDtypeStruct((KT, B, SW), F32)],
        scratch_shapes=[pltpu.VMEM((rows, D), F32), pltpu.VMEM((2, rows, SW2), F32)],
        compiler_params=_params(vmem, 1),
        name="s5_scan",
    )(xt, g, sc, sh, bmat, cmat, a_r, a_i, dsk, s0r, s0i)


def _s5_weights(a_re, a_im, b_re, b_im, c_re, c_im, d_skip, log_dt):
    G, N, P = b_re.shape
    KW = MXU_DIM_V7X
    GP = KW // P
    KT = G // GP
    ab_re, ab_im, bb_re, bb_im = _s5_discretize(a_re, a_im, log_dt, b_re, b_im)
    same_group = jnp.eye(GP, dtype=bool)
    bb = jnp.stack([bb_re, bb_im], axis=2).reshape(KT, GP, P, 2, 1, N)
    bmat = jnp.where(same_group[None, :, None, None, :, None], bb, 0.0)
    bmat = bmat.reshape(KT, KW, 2 * GP * N)
    cc = jnp.stack([c_re, -c_im]).reshape(2, KT, GP, P, N)
    cc = jnp.transpose(cc, (1, 0, 2, 4, 3)).reshape(KT, 2, GP, N, 1, P)
    cmat = jnp.where(same_group[None, None, :, None, :, None], cc, 0.0)
    cmat = cmat.reshape(KT, 2 * GP * N, KW)
    a_r = ab_re.reshape(KT, 1, GP * N)
    a_i = ab_im.reshape(KT, 1, GP * N)
    return bmat.astype(BF16), cmat.astype(BF16), a_r, a_i, d_skip.reshape(1, G * P)


def _trunk(x, mod, row0, st_C, st_n, st_m, st_conv, st_re, st_im, p):
    B, L, D = x.shape
    depth = p['g_norm'].shape[0]
    new_C, new_n, new_m, new_conv, new_re, new_im = [], [], [], [], [], []
    tiled = L >= ROW_TILE
    xt = None
    for i in range(depth):
        m6 = mod[i, row0:row0 + B].reshape(B, 6, 1, D)
        sh1, sc1, g1, sh2, sc2, g2 = (m6[:, j] for j in range(6))
        gn = p['g_norm'][i]
        gvec = lambda r: gn[r].reshape(1, D)
        kind, j = i % N_MIXERS, i // N_MIXERS
        if kind == 0:
            H = st_C.shape[2]
            nz = p['wA_in'].shape[-1] - 2 * H
            w_gates = jnp.pad(p['wA_in'][j, :, nz:], ((0, 0), (0, LANES - 2 * H)))
            b_gates = jnp.pad(p['bA_gates'][j], (0, LANES - 2 * H)).reshape(1, LANES)
            z, gates = _inproj(x, gvec(0), sc1, sh1, p['wA_in'], j, nz, w_gates, b_gates)
            x, C, n, m = _mlstm_mixer(z, gates, p['gA_hnorm'][j].reshape(1, -1), st_C, j,
                                      st_n[j], st_m[j], x, gvec(1), g1, p['wA_out'])
            new_C.append(C); new_n.append(n); new_m.append(m)
        elif kind == 1:
            z3 = _inproj(x, gvec(0), sc1, sh1, p['wB_in'], j, p['wB_in'].shape[-1])
            x, cv = _conv_mixer(z3, p['wB_conv'][j], st_conv[j], x, gvec(1), g1,
                                p['wB_out'], j)
            new_conv.append(cv)
        else:
            bmat, cmat, a_r, a_i, dsk = _s5_weights(
                p['s5_A_re'][j], p['s5_A_im'][j], p['s5_B_re'][j], p['s5_B_im'][j],
                p['s5_C_re'][j], p['s5_C_im'][j], p['s5_D'][j], p['s5_log_dt'][j])
            KT, _, SW2 = bmat.shape
            to_lanes = lambda s: jnp.swapaxes(s.reshape(B, KT, SW2 // 2), 0, 1)
            if xt is None:
                xt = jnp.swapaxes(x, 0, 1)
            yt, sr, si = _s5_core(xt, gvec(0), sc1.reshape(B, D),
                                  sh1.reshape(B, D), bmat, cmat, a_r, a_i, dsk,
                                  to_lanes(st_re[j]), to_lanes(st_im[j]))
            from_lanes = lambda s: jnp.swapaxes(s, 0, 1).reshape(st_re[j].shape)
            new_re.append(from_lanes(sr)); new_im.append(from_lanes(si))
            x = _outproj(yt if tiled else jnp.swapaxes(yt, 0, 1), x, gvec(1), g1,
                         p['wC_out'], j, glu=True, a_time_major=tiled)
        want_xt = tiled and i + 1 < depth and (i + 1) % N_MIXERS == 2
        x = _ffn(x, gvec(2), sc2, sh2, gvec(3), g2, p['w_ffn_gate'], p['w_ffn_up'],
                 p['w_ffn_down'], i, time_major_copy=want_xt)
        xt = None
        if want_xt:
            x, xt = x
    return (x, jnp.stack(new_C), jnp.stack(new_n), jnp.stack(new_m), jnp.stack(new_conv),
            jnp.stack(new_re), jnp.stack(new_im))


def kernel(x_prompt, x_sample, state_mlstm_C, state_mlstm_n, state_mlstm_m, state_conv,
           state_s5_re, state_s5_im, c_prompt, c_sample, w_mod, b_mod, g_norm, wA_in,
           bA_gates, gA_hnorm, wA_out, wB_in, wB_conv, wB_out, s5_A_re, s5_A_im, s5_B_re,
           s5_B_im, s5_C_re, s5_C_im, s5_D, s5_log_dt, wC_out, w_ffn_gate, w_ffn_up,
           w_ffn_down):
    cast = lambda w: w.astype(BF16)
    p = dict(g_norm=g_norm, wA_in=cast(wA_in), bA_gates=bA_gates, gA_hnorm=gA_hnorm,
             wA_out=cast(wA_out), wB_in=cast(wB_in), wB_conv=wB_conv, wB_out=cast(wB_out),
             s5_A_re=s5_A_re, s5_A_im=s5_A_im, s5_B_re=s5_B_re, s5_B_im=s5_B_im,
             s5_C_re=s5_C_re, s5_C_im=s5_C_im, s5_D=s5_D, s5_log_dt=s5_log_dt,
             wC_out=cast(wC_out), w_ffn_gate=cast(w_ffn_gate), w_ffn_up=cast(w_ffn_up),
             w_ffn_down=cast(w_ffn_down))
    bp = x_prompt.shape[0]
    mod = _modulation(jnp.concatenate([c_prompt, c_sample], axis=0), w_mod, b_mod)
    zeros = lambda s: jnp.zeros((s.shape[0], bp) + s.shape[2:], s.dtype)
    outs_p = _trunk(x_prompt, mod, 0, zeros(state_mlstm_C), zeros(state_mlstm_n),
                    zeros(state_mlstm_m), zeros(state_conv), zeros(state_s5_re),
                    zeros(state_s5_im), p)
    outs_s = _trunk(x_sample, mod, bp, state_mlstm_C, state_mlstm_n, state_mlstm_m,
                    state_conv, state_s5_re, state_s5_im, p)
    return (outs_p[0], outs_s[0]) + tuple(outs_p[1:]) + tuple(outs_s[1:])
```

```python
import functools

import jax
import jax.numpy as jnp
from jax import lax
from jax.experimental import pallas as pl
from jax.experimental.pallas import tpu as pltpu

F32 = jnp.float32
BF16 = jnp.bfloat16
EPS = 1e-6
N_MIXERS = 3

LANES = 128
SUBLANES = 8
MXU_DIM_V7X = 256
VMEM_BYTES_V7X = 64 * 1024 * 1024
VMEM_CAP = VMEM_BYTES_V7X - 6 * 1024 * 1024

ROW_TILE = 512
ROW_GROUP = MXU_DIM_V7X
FFN_ROW_TILE = 1024
INPROJ_ROW_TILE = 1024
INPROJ_COL_TILE = 2048
FFN_COL_TILE = 512
MOD_COL_TILE = 1024
MLSTM_CHUNK = 256
S5_CHUNK = 32
S5_LANE_GROUP = 4


def _params(vmem_bytes, n_grid):
    limit = int(min(VMEM_CAP, max(vmem_bytes * 5 // 4 + (4 << 20), 16 << 20)))
    return pltpu.CompilerParams(dimension_semantics=("arbitrary",) * n_grid,
                                vmem_limit_bytes=limit)


def _resident(block_shape, index_map):
    return pl.BlockSpec(block_shape, index_map, pipeline_mode=pl.Buffered(1))


def _row_blocking(B, L, tile=ROW_TILE):
    if L >= ROW_TILE:
        tile = min(tile, L)
        assert L % tile == 0
        return 1, tile
    assert L % SUBLANES == 0
    return B, L


def _norm_mod(x, g, sc, sh):
    ms = jnp.mean(x * x, axis=-1, keepdims=True)
    y = x * lax.rsqrt(ms + EPS) * g
    return y * (1.0 + sc) + sh


def _rms_gain(y, g):
    ms = jnp.mean(y * y, axis=-1, keepdims=True)
    return y * lax.rsqrt(ms + EPS) * g


def _row_groups(bt, tl):
    if bt != 1 or tl <= ROW_GROUP:
        return [(slice(0, tl), slice(0, bt * tl))]
    assert tl % ROW_GROUP == 0
    return [(slice(q * ROW_GROUP, (q + 1) * ROW_GROUP),) * 2 for q in range(tl // ROW_GROUP)]


def _mod_kernel(c_ref, w_ref, b_ref, o_ref):
    c = c_ref[...]
    sc = (c * jax.nn.sigmoid(c)).astype(BF16)
    o_ref[...] = jnp.dot(sc, w_ref[...].astype(BF16), preferred_element_type=F32) + b_ref[...]


def _modulation(c_all, w_mod, b_mod):
    depth, D, N = w_mod.shape
    R = c_all.shape[0]
    tn = min(MOD_COL_TILE, N)
    assert N % tn == 0
    vmem = 2 * D * tn * 4 + 2 * R * tn * 4 + R * D * 4
    return pl.pallas_call(
        _mod_kernel,
        grid=(depth, N // tn),
        in_specs=[_resident((R, D), lambda i, j: (0, 0)),
                  pl.BlockSpec((None, D, tn), lambda i, j: (i, 0, j)),
                  pl.BlockSpec((None, 1, tn), lambda i, j: (i, 0, j))],
        out_specs=pl.BlockSpec((None, R, tn), lambda i, j: (i, 0, j)),
        out_shape=jax.ShapeDtypeStruct((depth, R, N), F32),
        compiler_params=_params(vmem, 2),
        name="adaln_modulation",
    )(c_all, w_mod, b_mod.reshape(depth, 1, N))


def _inproj_kernel(x_ref, g_ref, sc_ref, sh_ref, w_ref, *rest, with_gates):
    if with_gates:
        wg_ref, bg_ref, z_ref, gates_ref, h_scr = rest
    else:
        z_ref, h_scr = rest
    bt, tl, D = x_ref.shape
    j = pl.program_id(2)

    @pl.when(j == 0)
    def _():
        for tsl, fr in _row_groups(bt, tl):
            h = _norm_mod(x_ref[:, tsl, :], g_ref[...], sc_ref[...], sh_ref[...])
            h2 = h.reshape(-1, D).astype(BF16)
            h_scr[fr, :] = h2
            z = jnp.dot(h2, w_ref[...], preferred_element_type=F32)
            z_ref[:, tsl, :] = z.reshape(bt, -1, z.shape[-1]).astype(z_ref.dtype)
            if with_gates:
                gates = jnp.dot(h2, wg_ref[...], preferred_element_type=F32) + bg_ref[...]
                gates_ref[:, tsl, :] = gates.reshape(bt, -1, gates.shape[-1])

    @pl.when(j > 0)
    def _():
        z = jnp.dot(h_scr[...], w_ref[...], preferred_element_type=F32)
        z_ref[...] = z.reshape(z_ref.shape).astype(z_ref.dtype)


def _inproj(x, g, sc, sh, w, layer, N, w_gates=None, b_gates=None):
    B, L, D = x.shape
    bt, tl = _row_blocking(B, L, INPROJ_ROW_TILE)
    rows = bt * tl
    tn = INPROJ_COL_TILE
    while N % tn:
        tn //= 2
    assert tn % LANES == 0
    with_gates = w_gates is not None
    row_map = lambda b, l, j: (b, l, 0)
    mod_map = lambda b, l, j: (b, 0, 0)
    in_specs = [pl.BlockSpec((bt, tl, D), row_map),
                _resident((1, D), lambda b, l, j: (0, 0)),
                pl.BlockSpec((bt, 1, D), mod_map),
                pl.BlockSpec((bt, 1, D), mod_map),
                pl.BlockSpec((None, D, tn), lambda b, l, j: (layer, 0, j))]
    args = [x, g, sc, sh, w]
    out_specs = [pl.BlockSpec((bt, tl, tn), lambda b, l, j: (b, l, j))]
    out_shape = [jax.ShapeDtypeStruct((B, L, N), BF16)]
    vmem = 2 * rows * D * 4 + 2 * D * tn * 2 + 2 * rows * tn * 2 + rows * D * 2
    if with_gates:
        in_specs += [_resident((D, LANES), lambda b, l, j: (0, 0)),
                     _resident((1, LANES), lambda b, l, j: (0, 0))]
        args += [w_gates, b_gates]
        out_specs.append(pl.BlockSpec((bt, tl, LANES), row_map))
        out_shape.append(jax.ShapeDtypeStruct((B, L, LANES), F32))
        vmem += D * LANES * 2 + 2 * rows * LANES * 4
    out = pl.pallas_call(
        functools.partial(_inproj_kernel, with_gates=with_gates),
        grid=(B // bt, L // tl, N // tn),
        in_specs=in_specs,
        out_specs=out_specs,
        out_shape=out_shape,
        scratch_shapes=[pltpu.VMEM((rows, D), BF16)],
        compiler_params=_params(vmem, 3),
        name="norm_mod_inproj",
    )(*args)
    return out if with_gates else out[0]


def _ffn_kernel(x_ref, g2_ref, sc_ref, sh_ref, g3_ref, gate_ref, wg_ref, wu_ref, wd_ref,
                o_ref, h_scr):
    bt, tl, D = x_ref.shape
    f = pl.program_id(2)
    last = pl.num_programs(2) - 1

    def partial_ffn(h2):
        gg = jnp.dot(h2, wg_ref[...], preferred_element_type=F32)
        uu = jnp.dot(h2, wu_ref[...], preferred_element_type=F32)
        act = (gg * jax.nn.sigmoid(gg) * uu).astype(BF16)
        return jnp.dot(act, wd_ref[...], preferred_element_type=F32).reshape(bt, -1, D)

    @pl.when(f == 0)
    def _():
        for tsl, fr in _row_groups(bt, tl):
            h = _norm_mod(x_ref[:, tsl, :], g2_ref[...], sc_ref[...], sh_ref[...])
            h2 = h.reshape(-1, D).astype(BF16)
            h_scr[fr, :] = h2
            o_ref[:, tsl, :] = partial_ffn(h2)

    @pl.when(jnp.logical_and(f > 0, f < last))
    def _():
        o_ref[...] += partial_ffn(h_scr[...])

    @pl.when(f == last)
    def _():
        for tsl, fr in _row_groups(bt, tl):
            y = o_ref[:, tsl, :] + partial_ffn(h_scr[fr, :])
            o_ref[:, tsl, :] = x_ref[:, tsl, :] + gate_ref[...] * _rms_gain(y, g3_ref[...])


def _ffn(x, g2, sc, sh, g3, gate, wg, wu, wd, layer):
    B, L, D = x.shape
    F = wg.shape[-1]
    bt, tl = _row_blocking(B, L, FFN_ROW_TILE)
    rows = bt * tl
    tf = FFN_COL_TILE
    assert F % tf == 0 and F // tf >= 2
    row_map = lambda b, l, f: (b, l, 0)
    mod_map = lambda b, l, f: (b, 0, 0)
    vec = lambda: _resident((1, D), lambda b, l, f: (0, 0))
    vmem = 4 * rows * D * 4 + 3 * 2 * D * tf * 2 + rows * D * 2 + 3 * rows * tf * 4
    return pl.pallas_call(
        _ffn_kernel,
        grid=(B // bt, L // tl, F // tf),
        in_specs=[pl.BlockSpec((bt, tl, D), row_map), vec(),
                  pl.BlockSpec((bt, 1, D), mod_map), pl.BlockSpec((bt, 1, D), mod_map),
                  vec(), pl.BlockSpec((bt, 1, D), mod_map),
                  pl.BlockSpec((None, D, tf), lambda b, l, f: (layer, 0, f)),
                  pl.BlockSpec((None, D, tf), lambda b, l, f: (layer, 0, f)),
                  pl.BlockSpec((None, tf, D), lambda b, l, f: (layer, f, 0))],
        out_specs=pl.BlockSpec((bt, tl, D), row_map),
        out_shape=jax.ShapeDtypeStruct((B, L, D), F32),
        scratch_shapes=[pltpu.VMEM((rows, D), BF16)],
        compiler_params=_params(vmem, 3),
        name="swiglu_ffn",
    )(x, g2, sc, sh, g3, gate, wg, wu, wd)


def _log_sigmoid(x):
    return -(jnp.maximum(-x, 0.0) + jnp.log1p(jnp.exp(-jnp.abs(x))))


def _cumsum_rows(x):
    n = x.shape[0]
    row = lax.broadcasted_iota(jnp.int32, x.shape, 0)
    s = 1
    while s < n:
        x = x + jnp.where(row >= s, pltpu.roll(x, s, 0), 0.0)
        s *= 2
    return x


def _mlstm_kernel(q_ref, k_ref, v_ref, o_ref, gt_ref, ghn_ref, c0_ref, n0_ref, m0_ref,
                  x_ref, g1_ref, gate_ref, wout_ref, xo_ref, c_ref, n_ref, m_ref, *, valid_len):
    H, DK, DV = c_ref.shape
    Lc = q_ref.shape[0]
    scale = DK ** -0.5

    @pl.when(pl.program_id(1) == 0)
    def _():
        c_ref[...] = c0_ref[...]
        n_ref[...] = n0_ref[...]
        m_ref[...] = m0_ref[...]

    gl = gt_ref[...]
    li_all = gl
    lf_all = _log_sigmoid(gl)
    if valid_len < Lc:
        valid = lax.broadcasted_iota(jnp.int32, gl.shape, 0) < valid_len
        li_all = jnp.where(valid, li_all, -jnp.inf)
        lf_all = jnp.where(valid, lf_all, 0.0)
    b_all = _cumsum_rows(lf_all)
    causal = (lax.broadcasted_iota(jnp.int32, (Lc, Lc), 0)
              >= lax.broadcasted_iota(jnp.int32, (Lc, Lc), 1))

    heads = range(H)
    q = [q_ref[:, h * DK:(h + 1) * DK] for h in heads]
    k = [k_ref[:, h * DK:(h + 1) * DK] for h in heads]
    v = [v_ref[:, h * DV:(h + 1) * DV] for h in heads]
    C = [c_ref[h] for h in heads]
    n = [n_ref[h] for h in heads]

    qk = [lax.dot_general(q[h], k[h], (((1,), (1,)), ((), ())), preferred_element_type=F32)
          for h in heads]
    qC = [jnp.dot(q[h], C[h].astype(BF16), preferred_element_type=F32) for h in heads]
    b, g, inter, m_t, w = [], [], [], [], []
    for h in heads:
        b.append(b_all[:, H + h:H + h + 1])
        g.append(li_all[:, h:h + 1] - b[h])
        g_row = jnp.transpose(jnp.broadcast_to(g[h], (Lc, LANES)))[0:1, :]
        dmat = jnp.where(causal, b[h] + g_row, -jnp.inf)
        inter.append(b[h] + m_ref[h])
        m_t.append(jnp.maximum(inter[h], jnp.max(dmat, axis=-1, keepdims=True)))
        w.append(jnp.exp(dmat - m_t[h]))

    sv, den_s, kw = [], [], []
    for h in heads:
        s = qk[h] * scale * w[h]
        den_s.append(jnp.sum(s, axis=-1, keepdims=True))
        sv.append(jnp.dot(s.astype(BF16), v[h], preferred_element_type=F32))
        m_new = m_t[h][Lc - 1:Lc, :]
        w_last = jnp.exp(b[h][Lc - 1:Lc, :] + g[h] - m_new)
        decay = jnp.exp(inter[h][Lc - 1:Lc, :] - m_new)
        kw.append(k[h].astype(F32) * w_last)
        c_ref[h] = decay * C[h] + lax.dot_general(
            kw[h].astype(BF16), v[h], (((0,), (0,)), ((), ())), preferred_element_type=F32)
        n_ref[h] = decay * n[h] + jnp.sum(kw[h], axis=0, keepdims=True)
        m_ref[h] = m_new

    y = None
    for h in heads:
        inter_w = jnp.exp(inter[h] - m_t[h]) * scale
        num = sv[h] + inter_w * qC[h]
        den = den_s[h] + inter_w * jnp.sum(q[h].astype(F32) * n[h], axis=-1, keepdims=True)
        floor = jnp.maximum(jnp.abs(den), jnp.exp(-m_t[h]))
        hh = num * (1.0 / floor)
        hn = _rms_gain(hh, ghn_ref[:, h * DV:(h + 1) * DV])
        og = o_ref[:, h * DV:(h + 1) * DV].astype(F32)
        a = (hn * jax.nn.sigmoid(og)).astype(BF16)
        yh = jnp.dot(a, wout_ref[h * DV:(h + 1) * DV, :], preferred_element_type=F32)
        y = yh if y is None else y + yh


    xo_ref[...] = x_ref[...] + gate_ref[...] * _rms_gain(y, g1_ref[...])


def _mlstm_mixer(z, gates, ghn, C0, layer, n0, m0, x, g1, gate, wout):
    B, L, D = x.shape
    _, _, H, DK, DV = C0.shape
    HK, HV = H * DK, H * DV
    assert HV == 2 * HK and z.shape[-1] == 2 * HK + 2 * HV
    valid_len = L
    if L >= MLSTM_CHUNK:
        Lc = MLSTM_CHUNK
        assert L % Lc == 0
    else:
        Lc = LANES
        pad = ((0, 0), (0, Lc - L), (0, 0))
        z, gates, x = jnp.pad(z, pad), jnp.pad(gates, pad), jnp.pad(x, pad)
    Lp = z.shape[1]
    blk = lambda width, idx: pl.BlockSpec((None, Lc, width), lambda b, c: (b, c, idx))
    st3 = lambda d1, d2: pl.BlockSpec((None, H, d1, d2), lambda b, c: (b, 0, 0, 0))
    vmem = (2 * Lc * (2 * HK + 2 * HV) * 2 + 2 * Lc * LANES * 4 + 4 * Lc * D * 4
            + 4 * H * DK * DV * 4 + 12 * Lc * Lc * 4 + 8 * Lc * DV * 4 + 2 * DK * DV * 4
            + HV * D * 2 + 3 * Lc * D * 4)
    xo, C, n, m = pl.pallas_call(
        functools.partial(_mlstm_kernel, valid_len=valid_len),
        grid=(B, Lp // Lc),
        in_specs=[blk(HK, 0), blk(HK, 1), blk(HV, 1), blk(HV, 2), blk(LANES, 0),
                  _resident((1, HV), lambda b, c: (0, 0)),
                  pl.BlockSpec((None, None, H, DK, DV), lambda b, c: (layer, b, 0, 0, 0)),
                  st3(1, DK), st3(1, 1),
                  blk(D, 0), _resident((1, D), lambda b, c: (0, 0)),
                  pl.BlockSpec((None, 1, D), lambda b, c: (b, 0, 0)),
                  _resident((None, HV, D), lambda b, c: (layer, 0, 0))],
        out_specs=[blk(D, 0), st3(DK, DV), st3(1, DK), st3(1, 1)],
        out_shape=[jax.ShapeDtypeStruct((B, Lp, D), F32),
                   jax.ShapeDtypeStruct((B, H, DK, DV), F32),
                   jax.ShapeDtypeStruct((B, H, 1, DK), F32),
                   jax.ShapeDtypeStruct((B, H, 1, 1), F32)],
        compiler_params=_params(vmem, 2),
        name="mlstm_mixer",
    )(z, z, z, z, gates, ghn, C0, n0.reshape(B, H, 1, DK), m0.reshape(B, H, 1, 1),
      x, g1, gate, wout)
    return xo[:, :L], C, n.reshape(B, H, DK), m.reshape(B, H)


def _conv_kernel(gb_ref, gc_ref, u_ref, w_ref, prev_ref, x_ref, g1_ref, gate_ref, wout_ref,
                 xo_ref, st_ref):
    tl, D = gc_ref.shape
    W = w_ref.shape[0]

    @pl.when(pl.program_id(1) == 0)
    def _():
        st_ref[...] = prev_ref[...]

    z = gc_ref[...].astype(F32) * u_ref[...].astype(F32)
    row = lax.broadcasted_iota(jnp.int32, (tl, D), 0)
    conv = z * w_ref[W - 1:W, :]
    for d in range(1, W):
        zd = pltpu.roll(z, d, 0)
        for r in range(d):
            zd = jnp.where(row == r, st_ref[W - 1 - d + r:W - d + r, :], zd)
        conv = conv + zd * w_ref[W - 1 - d:W - d, :]
    a = (gb_ref[...].astype(F32) * conv).astype(BF16)
    st_ref[...] = z[tl - (W - 1):, :]
    y = jnp.dot(a, wout_ref[...], preferred_element_type=F32)
    xo_ref[...] = x_ref[...] + gate_ref[...] * _rms_gain(y, g1_ref[...])


def _conv_mixer(z3, w_conv, prev, x, g1, gate, wout, layer):
    B, L, D = x.shape
    W = w_conv.shape[1]
    tl = min(L, ROW_TILE)
    assert L % tl == 0 and tl >= W - 1 and z3.shape[-1] == 3 * D
    blk = lambda idx: pl.BlockSpec((None, tl, D), lambda b, l: (b, l, idx))
    st = pl.BlockSpec((None, W - 1, D), lambda b, l: (b, 0, 0))
    vmem = 2 * 3 * tl * D * 2 + 4 * tl * D * 4 + D * D * 2 + 6 * tl * D * 4
    return pl.pallas_call(
        _conv_kernel,
        grid=(B, L // tl),
        in_specs=[blk(0), blk(1), blk(2), _resident((W, D), lambda b, l: (0, 0)), st,
                  blk(0), _resident((1, D), lambda b, l: (0, 0)),
                  pl.BlockSpec((None, 1, D), lambda b, l: (b, 0, 0)),
                  _resident((None, D, D), lambda b, l: (layer, 0, 0))],
        out_specs=[blk(0), st],
        out_shape=[jax.ShapeDtypeStruct((B, L, D), F32),
                   jax.ShapeDtypeStruct((B, W - 1, D), prev.dtype)],
        compiler_params=_params(vmem, 2),
        name="conv_mixer",
    )(z3, z3, z3, jnp.transpose(w_conv), prev, x, g1, gate, wout)


def _s5_disc_kernel(ar_ref, ai_ref, ldt_ref, br_ref, bi_ref, cr_ref, ci_ref, tn_ref, tp_ref,
                    abr_ref, abi_ref, bm_ref, cm_ref):
    G, P, N = br_ref.shape
    KT, KW, SW2 = bm_ref.shape
    GP, SW = KW // P, SW2 // 2
    dt = jnp.exp(ldt_ref[...])
    lr, lim = ar_ref[...], ai_ref[...]
    mag = jnp.exp(lr * dt)
    ab_re, ab_im = mag * jnp.cos(lim * dt), mag * jnp.sin(lim * dt)
    den = lr * lr + lim * lim
    nr = ab_re - 1.0
    fr = (nr * lr + ab_im * lim) / den
    fi = (ab_im * lr - nr * lim) / den
    abr_ref[...] = ab_re
    abi_ref[...] = ab_im
    br, bi = br_ref[...], bi_ref[...]
    bb = (fr[:, None, :] * br - fi[:, None, :] * bi, fr[:, None, :] * bi + fi[:, None, :] * br)
    cc = (cr_ref[...], -ci_ref[...])

    b_same = (lax.broadcasted_iota(jnp.int32, (KW, SW), 0) // P
              == lax.broadcasted_iota(jnp.int32, (KW, SW), 1) // N)
    c_same = (lax.broadcasted_iota(jnp.int32, (SW, KW), 0) // N
              == lax.broadcasted_iota(jnp.int32, (SW, KW), 1) // P)
    for kt in range(KT):
        for ri in range(2):
            bk = bb[ri][kt * GP:(kt + 1) * GP].reshape(KW, N).astype(BF16)
            spread = jnp.dot(bk, tn_ref[...], preferred_element_type=F32)
            bm_ref[kt, :, ri * SW:(ri + 1) * SW] = jnp.where(b_same, spread, 0.0).astype(BF16)
            ck = cc[ri][kt * GP:(kt + 1) * GP].reshape(SW, P).astype(BF16)
            spread = jnp.dot(ck, tp_ref[...], preferred_element_type=F32)
            cm_ref[kt, ri * SW:(ri + 1) * SW, :] = jnp.where(c_same, spread, 0.0).astype(BF16)


def _s5_kernel(x_ref, g_ref, sc_ref, sh_ref, bm_ref, cm_ref, ar_ref, ai_ref, dsk_ref,
               s0r_ref, s0i_ref, g1_ref, gate_ref, wa_ref, wb_ref,
               xo_ref, sr_ref, si_ref, h_scr, bu_scr, y_scr):
    B, T, D = x_ref.shape
    KT, KW, SW2 = bm_ref.shape
    SW = SW2 // 2
    rows = T * B
    LG = S5_LANE_GROUP * LANES

    @pl.when(pl.program_id(0) == 0)
    def _():
        sr_ref[...] = s0r_ref[...]
        si_ref[...] = s0i_ref[...]

    xt = pltpu.einshape("btd->tbd", x_ref[...])
    h_scr[...] = _norm_mod(xt, g_ref[...], sc_ref[...], sh_ref[...]).reshape(rows, D)

    for kt in range(KT):
        cols = slice(kt * KW, (kt + 1) * KW)
        bu = bu_scr.at[kt % 2]
        hk = h_scr[:, cols]
        bu[...] = jnp.dot(hk.astype(BF16), bm_ref[kt], preferred_element_type=F32)
        for lg in range(SW // LG):
            re_cols = slice(lg * LG, (lg + 1) * LG)
            im_cols = slice(SW + lg * LG, SW + (lg + 1) * LG)
            a_r = jnp.broadcast_to(ar_ref[kt, :, re_cols], (B, LG))
            a_i = jnp.broadcast_to(ai_ref[kt, :, re_cols], (B, LG))
            xr, xi = sr_ref[kt, :, re_cols], si_ref[kt, :, re_cols]
            for t in range(T):
                r = slice(t * B, (t + 1) * B)
                xr, xi = (a_r * xr - a_i * xi + bu[r, re_cols],
                          a_r * xi + a_i * xr + bu[r, im_cols])
                bu[r, re_cols] = xr
                bu[r, im_cols] = xi
            sr_ref[kt, :, re_cols] = xr
            si_ref[kt, :, re_cols] = xi
        yk = jnp.dot(bu[...].astype(BF16), cm_ref[kt], preferred_element_type=F32)
        yk = yk + dsk_ref[:, cols] * hk
        y_scr[:, cols] = jax.nn.gelu(yk).astype(y_scr.dtype)

    yg = y_scr[...]
    z = (jnp.dot(yg, wa_ref[...], preferred_element_type=F32)
         * jax.nn.sigmoid(jnp.dot(yg, wb_ref[...], preferred_element_type=F32)))
    out = xt + gate_ref[...] * _rms_gain(z, g1_ref[...]).reshape(T, B, D)
    xo_ref[...] = pltpu.einshape("tbd->btd", out)


def _s5_mixer(x, g, sc, sh, bmat, cmat, a_r, a_i, dsk, s0r, s0i, g1, gate, wout, layer):
    B, L, D = x.shape
    KT, KW, SW2 = bmat.shape
    SW = SW2 // 2
    assert B == SUBLANES and SW % (S5_LANE_GROUP * LANES) == 0
    T = min(S5_CHUNK, L)
    assert L % T == 0
    rows = T * B
    c0 = lambda *shape: _resident(shape, lambda c: (0,) * len(shape))
    vmem = (4 * rows * D * 4 + 2 * KT * KW * SW2 * 2 + 2 * D * D * 2 + rows * D * 4
            + 2 * rows * SW2 * 4 + rows * D * 2 + 5 * rows * D * 4 + 6 * KT * B * SW * 4)
    return pl.pallas_call(
        _s5_kernel,
        grid=(L // T,),
        in_specs=[pl.BlockSpec((B, T, D), lambda c: (0, c, 0)), c0(1, D), c0(B, D), c0(B, D),
                  c0(KT, KW, SW2), c0(KT, SW2, KW), c0(KT, 1, SW), c0(KT, 1, SW), c0(1, D),
                  c0(KT, B, SW), c0(KT, B, SW), c0(1, D), c0(B, D),
                  _resident((None, D, D), lambda c: (layer, 0, 0)),
                  _resident((None, D, D), lambda c: (layer, 0, 1))],
        out_specs=[pl.BlockSpec((B, T, D), lambda c: (0, c, 0)),
                   pl.BlockSpec((KT, B, SW), lambda c: (0, 0, 0)),
                   pl.BlockSpec((KT, B, SW), lambda c: (0, 0, 0))],
        out_shape=[jax.ShapeDtypeStruct((B, L, D), F32),
                   jax.ShapeDtypeStruct((KT, B, SW), F32),
                   jax.ShapeDtypeStruct((KT, B, SW), F32)],
        scratch_shapes=[pltpu.VMEM((rows, D), F32), pltpu.VMEM((2, rows, SW2), F32),
                        pltpu.VMEM((rows, D), BF16)],
        compiler_params=_params(vmem, 1),
        name="s5_mixer",
    )(x, g, sc, sh, bmat, cmat, a_r, a_i, dsk, s0r, s0i, g1, gate, wout, wout)


def _s5_weights(a_re, a_im, b_re, b_im, c_re, c_im, d_skip, log_dt):
    G, N, P = b_re.shape
    KW = MXU_DIM_V7X
    GP = KW // P
    KT, SW = G // GP, GP * N
    full = lambda *shape: pl.BlockSpec(shape, lambda: (0,) * len(shape))
    tile_n = jnp.tile(jnp.eye(N, dtype=BF16), (1, GP))
    tile_p = jnp.tile(jnp.eye(P, dtype=BF16), (1, GP))
    ab_re, ab_im, bmat, cmat = pl.pallas_call(
        _s5_disc_kernel,
        in_specs=[full(G, N), full(G, N), full(G, 1), full(G, P, N), full(G, P, N),
                  full(G, N, P), full(G, N, P), full(N, SW), full(P, KW)],
        out_specs=[full(G, N), full(G, N), full(KT, KW, 2 * SW), full(KT, 2 * SW, KW)],
        out_shape=[jax.ShapeDtypeStruct((G, N), F32), jax.ShapeDtypeStruct((G, N), F32),
                   jax.ShapeDtypeStruct((KT, KW, 2 * SW), BF16),
                   jax.ShapeDtypeStruct((KT, 2 * SW, KW), BF16)],
        compiler_params=pltpu.CompilerParams(vmem_limit_bytes=VMEM_CAP),
        name="s5_discretize",
    )(a_re, a_im, log_dt.reshape(G, 1), jnp.swapaxes(b_re, 1, 2), jnp.swapaxes(b_im, 1, 2),
      jnp.swapaxes(c_re, 1, 2), jnp.swapaxes(c_im, 1, 2), tile_n, tile_p)
    return (bmat, cmat, ab_re.reshape(KT, 1, SW), ab_im.reshape(KT, 1, SW),
            d_skip.reshape(1, G * P))


def _trunk(x, mod, row0, st_C, st_n, st_m, st_conv, st_re, st_im, p):
    B, L, D = x.shape
    depth = p['g_norm'].shape[0]
    new_C, new_n, new_m, new_conv, new_re, new_im = [], [], [], [], [], []
    for i in range(depth):
        m6 = mod[i, row0:row0 + B].reshape(B, 6, 1, D)
        sh1, sc1, g1, sh2, sc2, g2 = (m6[:, j] for j in range(6))
        gn = p['g_norm'][i]
        gvec = lambda r: gn[r].reshape(1, D)
        kind, j = i % N_MIXERS, i // N_MIXERS
        if kind == 0:
            H = st_C.shape[2]
            nz = p['wA_in'].shape[-1] - 2 * H
            w_gates = jnp.pad(p['wA_in'][j, :, nz:], ((0, 0), (0, LANES - 2 * H)))
            b_gates = jnp.pad(p['bA_gates'][j], (0, LANES - 2 * H)).reshape(1, LANES)
            z, gates = _inproj(x, gvec(0), sc1, sh1, p['wA_in'], j, nz, w_gates, b_gates)
            x, C, n, m = _mlstm_mixer(z, gates, p['gA_hnorm'][j].reshape(1, -1), st_C, j,
                                      st_n[j], st_m[j], x, gvec(1), g1, p['wA_out'])
            new_C.append(C); new_n.append(n); new_m.append(m)
        elif kind == 1:
            z3 = _inproj(x, gvec(0), sc1, sh1, p['wB_in'], j, p['wB_in'].shape[-1])
            x, cv = _conv_mixer(z3, p['wB_conv'][j], st_conv[j], x, gvec(1), g1,
                                p['wB_out'], j)
            new_conv.append(cv)
        else:
            bmat, cmat, a_r, a_i, dsk = _s5_weights(
                p['s5_A_re'][j], p['s5_A_im'][j], p['s5_B_re'][j], p['s5_B_im'][j],
                p['s5_C_re'][j], p['s5_C_im'][j], p['s5_D'][j], p['s5_log_dt'][j])
            KT, _, SW2 = bmat.shape
            to_lanes = lambda s: jnp.swapaxes(s.reshape(B, KT, SW2 // 2), 0, 1)
            x, sr, si = _s5_mixer(x, gvec(0), sc1.reshape(B, D), sh1.reshape(B, D), bmat, cmat,
                                  a_r, a_i, dsk, to_lanes(st_re[j]), to_lanes(st_im[j]),
                                  gvec(1), g1.reshape(B, D), p['wC_out'], j)
            from_lanes = lambda s: jnp.swapaxes(s, 0, 1).reshape(st_re[j].shape)
            new_re.append(from_lanes(sr)); new_im.append(from_lanes(si))
        x = _ffn(x, gvec(2), sc2, sh2, gvec(3), g2, p['w_ffn_gate'], p['w_ffn_up'],
                 p['w_ffn_down'], i)
    return (x, jnp.stack(new_C), jnp.stack(new_n), jnp.stack(new_m), jnp.stack(new_conv),
            jnp.stack(new_re), jnp.stack(new_im))


def kernel(x_prompt, x_sample, state_mlstm_C, state_mlstm_n, state_mlstm_m, state_conv,
           state_s5_re, state_s5_im, c_prompt, c_sample, w_mod, b_mod, g_norm, wA_in,
           bA_gates, gA_hnorm, wA_out, wB_in, wB_conv, wB_out, s5_A_re, s5_A_im, s5_B_re,
           s5_B_im, s5_C_re, s5_C_im, s5_D, s5_log_dt, wC_out, w_ffn_gate, w_ffn_up,
           w_ffn_down):
    cast = lambda w: w.astype(BF16)
    p = dict(g_norm=g_norm, wA_in=cast(wA_in), bA_gates=bA_gates, gA_hnorm=gA_hnorm,
             wA_out=cast(wA_out), wB_in=cast(wB_in), wB_conv=wB_conv, wB_out=cast(wB_out),
             s5_A_re=s5_A_re, s5_A_im=s5_A_im, s5_B_re=s5_B_re, s5_B_im=s5_B_im,
             s5_C_re=s5_C_re, s5_C_im=s5_C_im, s5_D=s5_D, s5_log_dt=s5_log_dt,
             wC_out=cast(wC_out), w_ffn_gate=cast(w_ffn_gate), w_ffn_up=cast(w_ffn_up),
             w_ffn_down=cast(w_ffn_down))
    bp = x_prompt.shape[0]
    mod = _modulation(jnp.concatenate([c_prompt, c_sample], axis=0), w_mod, b_mod)
    zeros = lambda s: jnp.zeros((s.shape[0], bp) + s.shape[2:], s.dtype)
    outs_p = _trunk(x_prompt, mod, 0, zeros(state_mlstm_C), zeros(state_mlstm_n),
                    zeros(state_mlstm_m), zeros(state_conv), zeros(state_s5_re),
                    zeros(state_s5_im), p)
    outs_s = _trunk(x_sample, mod, bp, state_mlstm_C, state_mlstm_n, state_mlstm_m,
                    state_conv, state_s5_re, state_s5_im, p)
    return (outs_p[0], outs_s[0]) + tuple(outs_p[1:]) + tuple(outs_s[1:])
```

```python
import functools

import jax
import jax.numpy as jnp
from jax import lax
from jax.experimental import pallas as pl
from jax.experimental.pallas import tpu as pltpu

F32 = jnp.float32
BF16 = jnp.bfloat16
EPS = 1e-6
N_MIXERS = 3

LANES = 128
SUBLANES = 8
MXU_DIM_V7X = 256
VMEM_BYTES_V7X = 64 * 1024 * 1024
VMEM_CAP = VMEM_BYTES_V7X - 6 * 1024 * 1024

ROW_TILE = 512
ROW_GROUP = MXU_DIM_V7X
FFN_ROW_TILE = 1024
INPROJ_ROW_TILE = 1024
INPROJ_COL_TILE = 2048
FFN_COL_TILE = 512
MOD_COL_TILE = 1024
MLSTM_CHUNK = 256
S5_CHUNK = 32
S5_LANE_GROUP = 4


def _params(vmem_bytes, n_grid):
    limit = int(min(VMEM_CAP, max(vmem_bytes * 5 // 4 + (4 << 20), 16 << 20)))
    return pltpu.CompilerParams(dimension_semantics=("arbitrary",) * n_grid,
                                vmem_limit_bytes=limit)


def _resident(block_shape, index_map):
    return pl.BlockSpec(block_shape, index_map, pipeline_mode=pl.Buffered(1))


def _row_blocking(B, L, tile=ROW_TILE):
    if L >= ROW_TILE:
        tile = min(tile, L)
        assert L % tile == 0
        return 1, tile
    assert L % SUBLANES == 0
    return B, L


def _serpentine(n_inner, inner_tiles):
    def tile(b, l, j):
        odd = (b * inner_tiles + l) % 2
        return j + odd * (n_inner - 1 - 2 * j)
    return tile


def _norm_mod(x, g, sc, sh):
    ms = jnp.mean(x * x, axis=-1, keepdims=True)
    y = x * lax.rsqrt(ms + EPS) * g
    return y * (1.0 + sc) + sh


def _rms_gain(y, g):
    ms = jnp.mean(y * y, axis=-1, keepdims=True)
    return y * lax.rsqrt(ms + EPS) * g


def _row_groups(bt, tl):
    if bt != 1 or tl <= ROW_GROUP:
        return [(slice(0, tl), slice(0, bt * tl))]
    assert tl % ROW_GROUP == 0
    return [(slice(q * ROW_GROUP, (q + 1) * ROW_GROUP),) * 2 for q in range(tl // ROW_GROUP)]


def _mod_kernel(c_ref, w_ref, b_ref, o_ref):
    c = c_ref[...]
    sc = (c * jax.nn.sigmoid(c)).astype(BF16)
    o_ref[...] = jnp.dot(sc, w_ref[...].astype(BF16), preferred_element_type=F32) + b_ref[...]


def _modulation(c_all, w_mod, b_mod):
    depth, D, N = w_mod.shape
    R = c_all.shape[0]
    tn = min(MOD_COL_TILE, N)
    assert N % tn == 0
    vmem = 2 * D * tn * 4 + 2 * R * tn * 4 + R * D * 4
    return pl.pallas_call(
        _mod_kernel,
        grid=(depth, N // tn),
        in_specs=[_resident((R, D), lambda i, j: (0, 0)),
                  pl.BlockSpec((None, D, tn), lambda i, j: (i, 0, j)),
                  pl.BlockSpec((None, 1, tn), lambda i, j: (i, 0, j))],
        out_specs=pl.BlockSpec((None, R, tn), lambda i, j: (i, 0, j)),
        out_shape=jax.ShapeDtypeStruct((depth, R, N), F32),
        compiler_params=_params(vmem, 2),
        name="adaln_modulation",
    )(c_all, w_mod, b_mod.reshape(depth, 1, N))


def _inproj_kernel(x_ref, g_ref, sc_ref, sh_ref, w_ref, *rest, with_gates):
    if with_gates:
        wg_ref, bg_ref, z_ref, gates_ref, h_scr = rest
    else:
        z_ref, h_scr = rest
    bt, tl, D = x_ref.shape
    j = pl.program_id(2)

    @pl.when(j == 0)
    def _():
        for tsl, fr in _row_groups(bt, tl):
            h = _norm_mod(x_ref[:, tsl, :], g_ref[...], sc_ref[...], sh_ref[...])
            h2 = h.reshape(-1, D).astype(BF16)
            h_scr[fr, :] = h2
            z = jnp.dot(h2, w_ref[...], preferred_element_type=F32)
            z_ref[:, tsl, :] = z.reshape(bt, -1, z.shape[-1]).astype(z_ref.dtype)
            if with_gates:
                gates = jnp.dot(h2, wg_ref[...], preferred_element_type=F32) + bg_ref[...]
                gates_ref[:, tsl, :] = gates.reshape(bt, -1, gates.shape[-1])

    @pl.when(j > 0)
    def _():
        z = jnp.dot(h_scr[...], w_ref[...], preferred_element_type=F32)
        z_ref[...] = z.reshape(z_ref.shape).astype(z_ref.dtype)


def _inproj(x, g, sc, sh, w, layer, N, w_gates=None, b_gates=None):
    B, L, D = x.shape
    bt, tl = _row_blocking(B, L, INPROJ_ROW_TILE)
    rows = bt * tl
    tn = INPROJ_COL_TILE
    while N % tn:
        tn //= 2
    assert tn % LANES == 0
    with_gates = w_gates is not None
    col = _serpentine(N // tn, L // tl)
    row_map = lambda b, l, j: (b, l, 0)
    mod_map = lambda b, l, j: (b, 0, 0)
    in_specs = [pl.BlockSpec((bt, tl, D), row_map),
                _resident((1, D), lambda b, l, j: (0, 0)),
                pl.BlockSpec((bt, 1, D), mod_map),
                pl.BlockSpec((bt, 1, D), mod_map),
                pl.BlockSpec((None, D, tn), lambda b, l, j: (layer, 0, col(b, l, j)))]
    args = [x, g, sc, sh, w]
    out_specs = [pl.BlockSpec((bt, tl, tn), lambda b, l, j: (b, l, col(b, l, j)))]
    out_shape = [jax.ShapeDtypeStruct((B, L, N), BF16)]
    vmem = 2 * rows * D * 4 + 2 * D * tn * 2 + 2 * rows * tn * 2 + rows * D * 2
    if with_gates:
        in_specs += [_resident((D, LANES), lambda b, l, j: (0, 0)),
                     _resident((1, LANES), lambda b, l, j: (0, 0))]
        args += [w_gates, b_gates]
        out_specs.append(pl.BlockSpec((bt, tl, LANES), row_map))
        out_shape.append(jax.ShapeDtypeStruct((B, L, LANES), F32))
        vmem += D * LANES * 2 + 2 * rows * LANES * 4
    out = pl.pallas_call(
        functools.partial(_inproj_kernel, with_gates=with_gates),
        grid=(B // bt, L // tl, N // tn),
        in_specs=in_specs,
        out_specs=out_specs,
        out_shape=out_shape,
        scratch_shapes=[pltpu.VMEM((rows, D), BF16)],
        compiler_params=_params(vmem, 3),
        name="norm_mod_inproj",
    )(*args)
    return out if with_gates else out[0]


def _ffn_kernel(x_ref, g2_ref, sc_ref, sh_ref, g3_ref, gate_ref, wg_ref, wu_ref, wd_ref,
                o_ref, h_scr):
    bt, tl, D = x_ref.shape
    f = pl.program_id(2)
    last = pl.num_programs(2) - 1

    def partial_ffn(h2):
        gg = jnp.dot(h2, wg_ref[...], preferred_element_type=F32)
        uu = jnp.dot(h2, wu_ref[...], preferred_element_type=F32)
        act = (gg * jax.nn.sigmoid(gg) * uu).astype(BF16)
        return jnp.dot(act, wd_ref[...], preferred_element_type=F32).reshape(bt, -1, D)

    @pl.when(f == 0)
    def _():
        for tsl, fr in _row_groups(bt, tl):
            h = _norm_mod(x_ref[:, tsl, :], g2_ref[...], sc_ref[...], sh_ref[...])
            h2 = h.reshape(-1, D).astype(BF16)
            h_scr[fr, :] = h2
            o_ref[:, tsl, :] = partial_ffn(h2)

    @pl.when(jnp.logical_and(f > 0, f < last))
    def _():
        o_ref[...] += partial_ffn(h_scr[...])

    @pl.when(f == last)
    def _():
        for tsl, fr in _row_groups(bt, tl):
            y = o_ref[:, tsl, :] + partial_ffn(h_scr[fr, :])
            o_ref[:, tsl, :] = x_ref[:, tsl, :] + gate_ref[...] * _rms_gain(y, g3_ref[...])


def _ffn(x, g2, sc, sh, g3, gate, wg, wu, wd, layer):
    B, L, D = x.shape
    F = wg.shape[-1]
    bt, tl = _row_blocking(B, L, FFN_ROW_TILE)
    rows = bt * tl
    tf = FFN_COL_TILE
    assert F % tf == 0 and F // tf >= 2
    hid = _serpentine(F // tf, L // tl)
    row_map = lambda b, l, f: (b, l, 0)
    mod_map = lambda b, l, f: (b, 0, 0)
    vec = lambda: _resident((1, D), lambda b, l, f: (0, 0))
    vmem = 4 * rows * D * 4 + 3 * 2 * D * tf * 2 + rows * D * 2 + 3 * rows * tf * 4
    return pl.pallas_call(
        _ffn_kernel,
        grid=(B // bt, L // tl, F // tf),
        in_specs=[pl.BlockSpec((bt, tl, D), row_map), vec(),
                  pl.BlockSpec((bt, 1, D), mod_map), pl.BlockSpec((bt, 1, D), mod_map),
                  vec(), pl.BlockSpec((bt, 1, D), mod_map),
                  pl.BlockSpec((None, D, tf), lambda b, l, f: (layer, 0, hid(b, l, f))),
                  pl.BlockSpec((None, D, tf), lambda b, l, f: (layer, 0, hid(b, l, f))),
                  pl.BlockSpec((None, tf, D), lambda b, l, f: (layer, hid(b, l, f), 0))],
        out_specs=pl.BlockSpec((bt, tl, D), row_map),
        out_shape=jax.ShapeDtypeStruct((B, L, D), F32),
        scratch_shapes=[pltpu.VMEM((rows, D), BF16)],
        compiler_params=_params(vmem, 3),
        name="swiglu_ffn",
    )(x, g2, sc, sh, g3, gate, wg, wu, wd)


def _log_sigmoid(x):
    return -(jnp.maximum(-x, 0.0) + jnp.log1p(jnp.exp(-jnp.abs(x))))


def _cumsum_rows(x):
    n = x.shape[0]
    row = lax.broadcasted_iota(jnp.int32, x.shape, 0)
    s = 1
    while s < n:
        x = x + jnp.where(row >= s, pltpu.roll(x, s, 0), 0.0)
        s *= 2
    return x


def _mlstm_kernel(q_ref, k_ref, v_ref, o_ref, gt_ref, ghn_ref, c0_ref, n0_ref, m0_ref,
                  x_ref, g1_ref, gate_ref, wout_ref, xo_ref, c_ref, n_ref, m_ref, *, valid_len):
    H, DK, DV = c_ref.shape
    Lc = q_ref.shape[0]
    scale = DK ** -0.5

    @pl.when(pl.program_id(1) == 0)
    def _():
        c_ref[...] = c0_ref[...]
        n_ref[...] = n0_ref[...]
        m_ref[...] = m0_ref[...]

    gl = gt_ref[...]
    li_all = gl
    lf_all = _log_sigmoid(gl)
    if valid_len < Lc:
        valid = lax.broadcasted_iota(jnp.int32, gl.shape, 0) < valid_len
        li_all = jnp.where(valid, li_all, -jnp.inf)
        lf_all = jnp.where(valid, lf_all, 0.0)
    b_all = _cumsum_rows(lf_all)
    causal = (lax.broadcasted_iota(jnp.int32, (Lc, Lc), 0)
              >= lax.broadcasted_iota(jnp.int32, (Lc, Lc), 1))

    heads = range(H)
    q = [q_ref[:, h * DK:(h + 1) * DK] for h in heads]
    k = [k_ref[:, h * DK:(h + 1) * DK] for h in heads]
    v = [v_ref[:, h * DV:(h + 1) * DV] for h in heads]
    C = [c_ref[h] for h in heads]
    n = [n_ref[h] for h in heads]

    qk = [lax.dot_general(q[h], k[h], (((1,), (1,)), ((), ())), preferred_element_type=F32)
          for h in heads]
    qC = [jnp.dot(q[h], C[h].astype(BF16), preferred_element_type=F32) for h in heads]
    b, g, inter, m_t, w = [], [], [], [], []
    for h in heads:
        b.append(b_all[:, H + h:H + h + 1])
        g.append(li_all[:, h:h + 1] - b[h])
        g_row = jnp.transpose(jnp.broadcast_to(g[h], (Lc, LANES)))[0:1, :]
        dmat = jnp.where(causal, b[h] + g_row, -jnp.inf)
        inter.append(b[h] + m_ref[h])
        m_t.append(jnp.maximum(inter[h], jnp.max(dmat, axis=-1, keepdims=True)))
        w.append(jnp.exp(dmat - m_t[h]))

    sv, den_s, kw = [], [], []
    for h in heads:
        s = qk[h] * scale * w[h]
        den_s.append(jnp.sum(s, axis=-1, keepdims=True))
        sv.append(jnp.dot(s.astype(BF16), v[h], preferred_element_type=F32))
        m_new = m_t[h][Lc - 1:Lc, :]
        w_last = jnp.exp(b[h][Lc - 1:Lc, :] + g[h] - m_new)
        decay = jnp.exp(inter[h][Lc - 1:Lc, :] - m_new)
        kw.append(k[h].astype(F32) * w_last)
        c_ref[h] = decay * C[h] + lax.dot_general(
            kw[h].astype(BF16), v[h], (((0,), (0,)), ((), ())), preferred_element_type=F32)
        n_ref[h] = decay * n[h] + jnp.sum(kw[h], axis=0, keepdims=True)
        m_ref[h] = m_new

    y = None
    for h in heads:
        inter_w = jnp.exp(inter[h] - m_t[h]) * scale
        num = sv[h] + inter_w * qC[h]
        den = den_s[h] + inter_w * jnp.sum(q[h].astype(F32) * n[h], axis=-1, keepdims=True)
        floor = jnp.maximum(jnp.abs(den), jnp.exp(-m_t[h]))
        hh = num * (1.0 / floor)
        hn = _rms_gain(hh, ghn_ref[:, h * DV:(h + 1) * DV])
        og = o_ref[:, h * DV:(h + 1) * DV].astype(F32)
        a = (hn * jax.nn.sigmoid(og)).astype(BF16)
        yh = jnp.dot(a, wout_ref[h * DV:(h + 1) * DV, :], preferred_element_type=F32)
        y = yh if y is None else y + yh


    xo_ref[...] = x_ref[...] + gate_ref[...] * _rms_gain(y, g1_ref[...])


def _mlstm_mixer(z, gates, ghn, C0, layer, n0, m0, x, g1, gate, wout):
    B, L, D = x.shape
    _, _, H, DK, DV = C0.shape
    HK, HV = H * DK, H * DV
    assert HV == 2 * HK and z.shape[-1] == 2 * HK + 2 * HV
    valid_len = L
    if L >= MLSTM_CHUNK:
        Lc = MLSTM_CHUNK
        assert L % Lc == 0
    else:
        Lc = LANES
        pad = ((0, 0), (0, Lc - L), (0, 0))
        z, gates, x = jnp.pad(z, pad), jnp.pad(gates, pad), jnp.pad(x, pad)
    Lp = z.shape[1]
    blk = lambda width, idx: pl.BlockSpec((None, Lc, width), lambda b, c: (b, c, idx))
    st3 = lambda d1, d2: pl.BlockSpec((None, H, d1, d2), lambda b, c: (b, 0, 0, 0))
    vmem = (2 * Lc * (2 * HK + 2 * HV) * 2 + 2 * Lc * LANES * 4 + 4 * Lc * D * 4
            + 4 * H * DK * DV * 4 + 12 * Lc * Lc * 4 + 8 * Lc * DV * 4 + 2 * DK * DV * 4
            + HV * D * 2 + 3 * Lc * D * 4)
    xo, C, n, m = pl.pallas_call(
        functools.partial(_mlstm_kernel, valid_len=valid_len),
        grid=(B, Lp // Lc),
        in_specs=[blk(HK, 0), blk(HK, 1), blk(HV, 1), blk(HV, 2), blk(LANES, 0),
                  _resident((1, HV), lambda b, c: (0, 0)),
                  pl.BlockSpec((None, None, H, DK, DV), lambda b, c: (layer, b, 0, 0, 0)),
                  st3(1, DK), st3(1, 1),
                  blk(D, 0), _resident((1, D), lambda b, c: (0, 0)),
                  pl.BlockSpec((None, 1, D), lambda b, c: (b, 0, 0)),
                  _resident((None, HV, D), lambda b, c: (layer, 0, 0))],
        out_specs=[blk(D, 0), st3(DK, DV), st3(1, DK), st3(1, 1)],
        out_shape=[jax.ShapeDtypeStruct((B, Lp, D), F32),
                   jax.ShapeDtypeStruct((B, H, DK, DV), F32),
                   jax.ShapeDtypeStruct((B, H, 1, DK), F32),
                   jax.ShapeDtypeStruct((B, H, 1, 1), F32)],
        compiler_params=_params(vmem, 2),
        name="mlstm_mixer",
    )(z, z, z, z, gates, ghn, C0, n0.reshape(B, H, 1, DK), m0.reshape(B, H, 1, 1),
      x, g1, gate, wout)
    return xo[:, :L], C, n.reshape(B, H, DK), m.reshape(B, H)


def _conv_kernel(gb_ref, gc_ref, u_ref, w_ref, prev_ref, x_ref, g1_ref, gate_ref, wout_ref,
                 xo_ref, st_ref):
    tl, D = gc_ref.shape
    W = w_ref.shape[0]

    @pl.when(pl.program_id(1) == 0)
    def _():
        st_ref[...] = prev_ref[...]

    z = gc_ref[...].astype(F32) * u_ref[...].astype(F32)
    row = lax.broadcasted_iota(jnp.int32, (tl, D), 0)
    conv = z * w_ref[W - 1:W, :]
    for d in range(1, W):
        zd = pltpu.roll(z, d, 0)
        for r in range(d):
            zd = jnp.where(row == r, st_ref[W - 1 - d + r:W - d + r, :], zd)
        conv = conv + zd * w_ref[W - 1 - d:W - d, :]
    a = (gb_ref[...].astype(F32) * conv).astype(BF16)
    st_ref[...] = z[tl - (W - 1):, :]
    y = jnp.dot(a, wout_ref[...], preferred_element_type=F32)
    xo_ref[...] = x_ref[...] + gate_ref[...] * _rms_gain(y, g1_ref[...])


def _conv_mixer(z3, w_conv, prev, x, g1, gate, wout, layer):
    B, L, D = x.shape
    W = w_conv.shape[1]
    tl = min(L, ROW_TILE)
    assert L % tl == 0 and tl >= W - 1 and z3.shape[-1] == 3 * D
    blk = lambda idx: pl.BlockSpec((None, tl, D), lambda b, l: (b, l, idx))
    st = pl.BlockSpec((None, W - 1, D), lambda b, l: (b, 0, 0))
    vmem = 2 * 3 * tl * D * 2 + 4 * tl * D * 4 + D * D * 2 + 6 * tl * D * 4
    return pl.pallas_call(
        _conv_kernel,
        grid=(B, L // tl),
        in_specs=[blk(0), blk(1), blk(2), _resident((W, D), lambda b, l: (0, 0)), st,
                  blk(0), _resident((1, D), lambda b, l: (0, 0)),
                  pl.BlockSpec((None, 1, D), lambda b, l: (b, 0, 0)),
                  _resident((None, D, D), lambda b, l: (layer, 0, 0))],
        out_specs=[blk(0), st],
        out_shape=[jax.ShapeDtypeStruct((B, L, D), F32),
                   jax.ShapeDtypeStruct((B, W - 1, D), prev.dtype)],
        compiler_params=_params(vmem, 2),
        name="conv_mixer",
    )(z3, z3, z3, jnp.transpose(w_conv), prev, x, g1, gate, wout)


def _s5_disc_kernel(ar_ref, ai_ref, ldt_ref, br_ref, bi_ref, cr_ref, ci_ref, tn_ref, tp_ref,
                    abr_ref, abi_ref, bm_ref, cm_ref):
    G, P, N = br_ref.shape
    KT, KW, SW2 = bm_ref.shape
    GP, SW = KW // P, SW2 // 2
    dt = jnp.exp(ldt_ref[...])
    lr, lim = ar_ref[...], ai_ref[...]
    mag = jnp.exp(lr * dt)
    ab_re, ab_im = mag * jnp.cos(lim * dt), mag * jnp.sin(lim * dt)
    den = lr * lr + lim * lim
    nr = ab_re - 1.0
    fr = (nr * lr + ab_im * lim) / den
    fi = (ab_im * lr - nr * lim) / den
    abr_ref[...] = ab_re
    abi_ref[...] = ab_im
    br, bi = br_ref[...], bi_ref[...]
    bb = (fr[:, None, :] * br - fi[:, None, :] * bi, fr[:, None, :] * bi + fi[:, None, :] * br)
    cc = (cr_ref[...], -ci_ref[...])

    b_same = (lax.broadcasted_iota(jnp.int32, (KW, SW), 0) // P
              == lax.broadcasted_iota(jnp.int32, (KW, SW), 1) // N)
    c_same = (lax.broadcasted_iota(jnp.int32, (SW, KW), 0) // N
              == lax.broadcasted_iota(jnp.int32, (SW, KW), 1) // P)
    for kt in range(KT):
        for ri in range(2):
            bk = bb[ri][kt * GP:(kt + 1) * GP].reshape(KW, N).astype(BF16)
            spread = jnp.dot(bk, tn_ref[...], preferred_element_type=F32)
            bm_ref[kt, :, ri * SW:(ri + 1) * SW] = jnp.where(b_same, spread, 0.0).astype(BF16)
            ck = cc[ri][kt * GP:(kt + 1) * GP].reshape(SW, P).astype(BF16)
            spread = jnp.dot(ck, tp_ref[...], preferred_element_type=F32)
            cm_ref[kt, ri * SW:(ri + 1) * SW, :] = jnp.where(c_same, spread, 0.0).astype(BF16)


def _s5_kernel(x_ref, g_ref, sc_ref, sh_ref, bm_ref, cm_ref, ar_ref, ai_ref, dsk_ref,
               s0r_ref, s0i_ref, g1_ref, gate_ref, wa_ref, wb_ref,
               xo_ref, sr_ref, si_ref, h_scr, bu_scr, y_scr):
    B, T, D = x_ref.shape
    KT, KW, SW2 = bm_ref.shape
    SW = SW2 // 2
    rows = T * B
    LG = S5_LANE_GROUP * LANES

    @pl.when(pl.program_id(0) == 0)
    def _():
        sr_ref[...] = s0r_ref[...]
        si_ref[...] = s0i_ref[...]

    xt = pltpu.einshape("btd->tbd", x_ref[...])
    h_scr[...] = _norm_mod(xt, g_ref[...], sc_ref[...], sh_ref[...]).reshape(rows, D)

    for kt in range(KT):
        cols = slice(kt * KW, (kt + 1) * KW)
        bu = bu_scr.at[kt % 2]
        hk = h_scr[:, cols]
        bu[...] = jnp.dot(hk.astype(BF16), bm_ref[kt], preferred_element_type=F32)
        for lg in range(SW // LG):
            re_cols = slice(lg * LG, (lg + 1) * LG)
            im_cols = slice(SW + lg * LG, SW + (lg + 1) * LG)
            a_r = jnp.broadcast_to(ar_ref[kt, :, re_cols], (B, LG))
            a_i = jnp.broadcast_to(ai_ref[kt, :, re_cols], (B, LG))
            xr, xi = sr_ref[kt, :, re_cols], si_ref[kt, :, re_cols]
            for t in range(T):
                r = slice(t * B, (t + 1) * B)
                xr, xi = (a_r * xr - a_i * xi + bu[r, re_cols],
                          a_r * xi + a_i * xr + bu[r, im_cols])
                bu[r, re_cols] = xr
                bu[r, im_cols] = xi
            sr_ref[kt, :, re_cols] = xr
            si_ref[kt, :, re_cols] = xi
        yk = jnp.dot(bu[...].astype(BF16), cm_ref[kt], preferred_element_type=F32)
        yk = yk + dsk_ref[:, cols] * hk
        y_scr[:, cols] = jax.nn.gelu(yk).astype(y_scr.dtype)

    yg = y_scr[...]
    z = (jnp.dot(yg, wa_ref[...], preferred_element_type=F32)
         * jax.nn.sigmoid(jnp.dot(yg, wb_ref[...], preferred_element_type=F32)))
    out = xt + gate_ref[...] * _rms_gain(z, g1_ref[...]).reshape(T, B, D)
    xo_ref[...] = pltpu.einshape("tbd->btd", out)


def _s5_mixer(x, g, sc, sh, bmat, cmat, a_r, a_i, dsk, s0r, s0i, g1, gate, wout, layer):
    B, L, D = x.shape
    KT, KW, SW2 = bmat.shape
    SW = SW2 // 2
    assert B == SUBLANES and SW % (S5_LANE_GROUP * LANES) == 0
    T = min(S5_CHUNK, L)
    assert L % T == 0
    rows = T * B
    c0 = lambda *shape: _resident(shape, lambda c: (0,) * len(shape))
    vmem = (4 * rows * D * 4 + 2 * KT * KW * SW2 * 2 + 2 * D * D * 2 + rows * D * 4
            + 2 * rows * SW2 * 4 + rows * D * 2 + 5 * rows * D * 4 + 6 * KT * B * SW * 4)
    return pl.pallas_call(
        _s5_kernel,
        grid=(L // T,),
        in_specs=[pl.BlockSpec((B, T, D), lambda c: (0, c, 0)), c0(1, D), c0(B, D), c0(B, D),
                  c0(KT, KW, SW2), c0(KT, SW2, KW), c0(KT, 1, SW), c0(KT, 1, SW), c0(1, D),
                  c0(KT, B, SW), c0(KT, B, SW), c0(1, D), c0(B, D),
                  _resident((None, D, D), lambda c: (layer, 0, 0)),
                  _resident((None, D, D), lambda c: (layer, 0, 1))],
        out_specs=[pl.BlockSpec((B, T, D), lambda c: (0, c, 0)),
                   pl.BlockSpec((KT, B, SW), lambda c: (0, 0, 0)),
                   pl.BlockSpec((KT, B, SW), lambda c: (0, 0, 0))],
        out_shape=[jax.ShapeDtypeStruct((B, L, D), F32),
                   jax.ShapeDtypeStruct((KT, B, SW), F32),
                   jax.ShapeDtypeStruct((KT, B, SW), F32)],
        scratch_shapes=[pltpu.VMEM((rows, D), F32), pltpu.VMEM((2, rows, SW2), F32),
                        pltpu.VMEM((rows, D), BF16)],
        compiler_params=_params(vmem, 1),
        name="s5_mixer",
    )(x, g, sc, sh, bmat, cmat, a_r, a_i, dsk, s0r, s0i, g1, gate, wout, wout)


def _s5_weights(a_re, a_im, b_re, b_im, c_re, c_im, d_skip, log_dt):
    G, N, P = b_re.shape
    KW = MXU_DIM_V7X
    GP = KW // P
    KT, SW = G // GP, GP * N
    full = lambda *shape: pl.BlockSpec(shape, lambda: (0,) * len(shape))
    tile_n = jnp.tile(jnp.eye(N, dtype=BF16), (1, GP))
    tile_p = jnp.tile(jnp.eye(P, dtype=BF16), (1, GP))
    ab_re, ab_im, bmat, cmat = pl.pallas_call(
        _s5_disc_kernel,
        in_specs=[full(G, N), full(G, N), full(G, 1), full(G, P, N), full(G, P, N),
                  full(G, N, P), full(G, N, P), full(N, SW), full(P, KW)],
        out_specs=[full(G, N), full(G, N), full(KT, KW, 2 * SW), full(KT, 2 * SW, KW)],
        out_shape=[jax.ShapeDtypeStruct((G, N), F32), jax.ShapeDtypeStruct((G, N), F32),
                   jax.ShapeDtypeStruct((KT, KW, 2 * SW), BF16),
                   jax.ShapeDtypeStruct((KT, 2 * SW, KW), BF16)],
        compiler_params=pltpu.CompilerParams(vmem_limit_bytes=VMEM_CAP),
        name="s5_discretize",
    )(a_re, a_im, log_dt.reshape(G, 1), jnp.swapaxes(b_re, 1, 2), jnp.swapaxes(b_im, 1, 2),
      jnp.swapaxes(c_re, 1, 2), jnp.swapaxes(c_im, 1, 2), tile_n, tile_p)
    return (bmat, cmat, ab_re.reshape(KT, 1, SW), ab_im.reshape(KT, 1, SW),
            d_skip.reshape(1, G * P))


def _trunk(x, mod, row0, st_C, st_n, st_m, st_conv, st_re, st_im, p):
    B, L, D = x.shape
    depth = p['g_norm'].shape[0]
    new_C, new_n, new_m, new_conv, new_re, new_im = [], [], [], [], [], []
    for i in range(depth):
        m6 = mod[i, row0:row0 + B].reshape(B, 6, 1, D)
        sh1, sc1, g1, sh2, sc2, g2 = (m6[:, j] for j in range(6))
        gn = p['g_norm'][i]
        gvec = lambda r: gn[r].reshape(1, D)
        kind, j = i % N_MIXERS, i // N_MIXERS
        if kind == 0:
            H = st_C.shape[2]
            nz = p['wA_in'].shape[-1] - 2 * H
            w_gates = jnp.pad(p['wA_in'][j, :, nz:], ((0, 0), (0, LANES - 2 * H)))
            b_gates = jnp.pad(p['bA_gates'][j], (0, LANES - 2 * H)).reshape(1, LANES)
            z, gates = _inproj(x, gvec(0), sc1, sh1, p['wA_in'], j, nz, w_gates, b_gates)
            x, C, n, m = _mlstm_mixer(z, gates, p['gA_hnorm'][j].reshape(1, -1), st_C, j,
                                      st_n[j], st_m[j], x, gvec(1), g1, p['wA_out'])
            new_C.append(C); new_n.append(n); new_m.append(m)
        elif kind == 1:
            z3 = _inproj(x, gvec(0), sc1, sh1, p['wB_in'], j, p['wB_in'].shape[-1])
            x, cv = _conv_mixer(z3, p['wB_conv'][j], st_conv[j], x, gvec(1), g1,
                                p['wB_out'], j)
            new_conv.append(cv)
        else:
            bmat, cmat, a_r, a_i, dsk = _s5_weights(
                p['s5_A_re'][j], p['s5_A_im'][j], p['s5_B_re'][j], p['s5_B_im'][j],
                p['s5_C_re'][j], p['s5_C_im'][j], p['s5_D'][j], p['s5_log_dt'][j])
            KT, _, SW2 = bmat.shape
            to_lanes = lambda s: jnp.swapaxes(s.reshape(B, KT, SW2 // 2), 0, 1)
            x, sr, si = _s5_mixer(x, gvec(0), sc1.reshape(B, D), sh1.reshape(B, D), bmat, cmat,
                                  a_r, a_i, dsk, to_lanes(st_re[j]), to_lanes(st_im[j]),
                                  gvec(1), g1.reshape(B, D), p['wC_out'], j)
            from_lanes = lambda s: jnp.swapaxes(s, 0, 1).reshape(st_re[j].shape)
            new_re.append(from_lanes(sr)); new_im.append(from_lanes(si))
        x = _ffn(x, gvec(2), sc2, sh2, gvec(3), g2, p['w_ffn_gate'], p['w_ffn_up'],
                 p['w_ffn_down'], i)
    return (x, jnp.stack(new_C), jnp.stack(new_n), jnp.stack(new_m), jnp.stack(new_conv),
            jnp.stack(new_re), jnp.stack(new_im))


def kernel(x_prompt, x_sample, state_mlstm_C, state_mlstm_n, state_mlstm_m, state_conv,
           state_s5_re, state_s5_im, c_prompt, c_sample, w_mod, b_mod, g_norm, wA_in,
           bA_gates, gA_hnorm, wA_out, wB_in, wB_conv, wB_out, s5_A_re, s5_A_im, s5_B_re,
           s5_B_im, s5_C_re, s5_C_im, s5_D, s5_log_dt, wC_out, w_ffn_gate, w_ffn_up,
           w_ffn_down):
    cast = lambda w: w.astype(BF16)
    p = dict(g_norm=g_norm, wA_in=cast(wA_in), bA_gates=bA_gates, gA_hnorm=gA_hnorm,
             wA_out=cast(wA_out), wB_in=cast(wB_in), wB_conv=wB_conv, wB_out=cast(wB_out),
             s5_A_re=s5_A_re, s5_A_im=s5_A_im, s5_B_re=s5_B_re, s5_B_im=s5_B_im,
             s5_C_re=s5_C_re, s5_C_im=s5_C_im, s5_D=s5_D, s5_log_dt=s5_log_dt,
             wC_out=cast(wC_out), w_ffn_gate=cast(w_ffn_gate), w_ffn_up=cast(w_ffn_up),
             w_ffn_down=cast(w_ffn_down))
    bp = x_prompt.shape[0]
    mod = _modulation(jnp.concatenate([c_prompt, c_sample], axis=0), w_mod, b_mod)
    zeros = lambda s: jnp.zeros((s.shape[0], bp) + s.shape[2:], s.dtype)
    outs_p = _trunk(x_prompt, mod, 0, zeros(state_mlstm_C), zeros(state_mlstm_n),
                    zeros(state_mlstm_m), zeros(state_conv), zeros(state_s5_re),
                    zeros(state_s5_im), p)
    outs_s = _trunk(x_sample, mod, bp, state_mlstm_C, state_mlstm_n, state_mlstm_m,
                    state_conv, state_s5_re, state_s5_im, p)
    return (outs_p[0], outs_s[0]) + tuple(outs_p[1:]) + tuple(outs_s[1:])
```

```python
import functools

import jax
import jax.numpy as jnp
from jax import lax
from jax.experimental import pallas as pl
from jax.experimental.pallas import tpu as pltpu

F32 = jnp.float32
BF16 = jnp.bfloat16
EPS = 1e-6
N_MIXERS = 3

LANES = 128
SUBLANES = 8
MXU_DIM_V7X = 256
VMEM_BYTES_V7X = 64 * 1024 * 1024
VMEM_CAP = VMEM_BYTES_V7X - 6 * 1024 * 1024

ROW_TILE = 512
ROW_GROUP = MXU_DIM_V7X
FFN_ROW_TILE = 1024
INPROJ_ROW_TILE = 1024
INPROJ_COL_TILE = 2048
FFN_COL_TILE = 512
MOD_COL_TILE = 1024
MLSTM_CHUNK = 256
S5_CHUNK = 32
S5_LANE_GROUP = 4


def _params(vmem_bytes, n_grid):
    limit = int(min(VMEM_CAP, max(vmem_bytes * 5 // 4 + (4 << 20), 16 << 20)))
    return pltpu.CompilerParams(dimension_semantics=("arbitrary",) * n_grid,
                                vmem_limit_bytes=limit)


def _resident(block_shape, index_map):
    return pl.BlockSpec(block_shape, index_map, pipeline_mode=pl.Buffered(1))


def _row_blocking(B, L, tile=ROW_TILE):
    if L >= ROW_TILE:
        tile = min(tile, L)
        assert L % tile == 0
        return 1, tile
    assert L % SUBLANES == 0
    return B, L


def _norm_mod(x, g, sc, sh):
    ms = jnp.mean(x * x, axis=-1, keepdims=True)
    y = x * lax.rsqrt(ms + EPS) * g
    return y * (1.0 + sc) + sh


def _rms_gain(y, g):
    ms = jnp.mean(y * y, axis=-1, keepdims=True)
    return y * lax.rsqrt(ms + EPS) * g


def _row_groups(bt, tl):
    if bt != 1 or tl <= ROW_GROUP:
        return [(slice(0, tl), slice(0, bt * tl))]
    assert tl % ROW_GROUP == 0
    return [(slice(q * ROW_GROUP, (q + 1) * ROW_GROUP),) * 2 for q in range(tl // ROW_GROUP)]


def _mod_kernel(c_ref, w_ref, b_ref, o_ref):
    c = c_ref[...]
    sc = (c * jax.nn.sigmoid(c)).astype(BF16)
    o_ref[...] = jnp.dot(sc, w_ref[...].astype(BF16), preferred_element_type=F32) + b_ref[...]


def _modulation(c_all, w_mod, b_mod):
    depth, D, N = w_mod.shape
    R = c_all.shape[0]
    tn = min(MOD_COL_TILE, N)
    assert N % tn == 0
    vmem = 2 * D * tn * 4 + 2 * R * tn * 4 + R * D * 4
    return pl.pallas_call(
        _mod_kernel,
        grid=(depth, N // tn),
        in_specs=[_resident((R, D), lambda i, j: (0, 0)),
                  pl.BlockSpec((None, D, tn), lambda i, j: (i, 0, j)),
                  pl.BlockSpec((None, 1, tn), lambda i, j: (i, 0, j))],
        out_specs=pl.BlockSpec((None, R, tn), lambda i, j: (i, 0, j)),
        out_shape=jax.ShapeDtypeStruct((depth, R, N), F32),
        compiler_params=_params(vmem, 2),
        name="adaln_modulation",
    )(c_all, w_mod, b_mod.reshape(depth, 1, N))


def _inproj_kernel(x_ref, g_ref, sc_ref, sh_ref, w_ref, *rest, with_gates):
    if with_gates:
        wg_ref, bg_ref, z_ref, gates_ref, h_scr = rest
    else:
        z_ref, h_scr = rest
    bt, tl, D = x_ref.shape
    j = pl.program_id(2)

    @pl.when(j == 0)
    def _():
        for tsl, fr in _row_groups(bt, tl):
            h = _norm_mod(x_ref[:, tsl, :], g_ref[...], sc_ref[...], sh_ref[...])
            h2 = h.reshape(-1, D).astype(BF16)
            h_scr[fr, :] = h2
            z = jnp.dot(h2, w_ref[...], preferred_element_type=F32)
            z_ref[:, tsl, :] = z.reshape(bt, -1, z.shape[-1]).astype(z_ref.dtype)
            if with_gates:
                gates = jnp.dot(h2, wg_ref[...], preferred_element_type=F32) + bg_ref[...]
                gates_ref[:, tsl, :] = gates.reshape(bt, -1, gates.shape[-1])

    @pl.when(j > 0)
    def _():
        z = jnp.dot(h_scr[...], w_ref[...], preferred_element_type=F32)
        z_ref[...] = z.reshape(z_ref.shape).astype(z_ref.dtype)


def _inproj(x, g, sc, sh, w, layer, N, w_gates=None, b_gates=None):
    B, L, D = x.shape
    bt, tl = _row_blocking(B, L, INPROJ_ROW_TILE)
    rows = bt * tl
    tn = INPROJ_COL_TILE
    while N % tn:
        tn //= 2
    assert tn % LANES == 0
    with_gates = w_gates is not None
    row_map = lambda b, l, j: (b, l, 0)
    mod_map = lambda b, l, j: (b, 0, 0)
    in_specs = [pl.BlockSpec((bt, tl, D), row_map),
                _resident((1, D), lambda b, l, j: (0, 0)),
                pl.BlockSpec((bt, 1, D), mod_map),
                pl.BlockSpec((bt, 1, D), mod_map),
                pl.BlockSpec((None, D, tn), lambda b, l, j: (layer, 0, j))]
    args = [x, g, sc, sh, w]
    out_specs = [pl.BlockSpec((bt, tl, tn), lambda b, l, j: (b, l, j))]
    out_shape = [jax.ShapeDtypeStruct((B, L, N), BF16)]
    vmem = 2 * rows * D * 4 + 2 * D * tn * 2 + 2 * rows * tn * 2 + rows * D * 2
    if with_gates:
        in_specs += [_resident((D, LANES), lambda b, l, j: (0, 0)),
                     _resident((1, LANES), lambda b, l, j: (0, 0))]
        args += [w_gates, b_gates]
        out_specs.append(pl.BlockSpec((bt, tl, LANES), row_map))
        out_shape.append(jax.ShapeDtypeStruct((B, L, LANES), F32))
        vmem += D * LANES * 2 + 2 * rows * LANES * 4
    out = pl.pallas_call(
        functools.partial(_inproj_kernel, with_gates=with_gates),
        grid=(B // bt, L // tl, N // tn),
        in_specs=in_specs,
        out_specs=out_specs,
        out_shape=out_shape,
        scratch_shapes=[pltpu.VMEM((rows, D), BF16)],
        compiler_params=_params(vmem, 3),
        name="norm_mod_inproj",
    )(*args)
    return out if with_gates else out[0]


def _ffn_kernel(x_ref, g2_ref, sc_ref, sh_ref, g3_ref, gate_ref, wg_ref, wu_ref, wd_ref,
                o_ref, h_scr):
    bt, tl, D = x_ref.shape
    f = pl.program_id(2)
    last = pl.num_programs(2) - 1

    def partial_ffn(h2):
        gg = jnp.dot(h2, wg_ref[...], preferred_element_type=F32)
        uu = jnp.dot(h2, wu_ref[...], preferred_element_type=F32)
        act = (gg * jax.nn.sigmoid(gg) * uu).astype(BF16)
        return jnp.dot(act, wd_ref[...], preferred_element_type=F32).reshape(bt, -1, D)

    @pl.when(f == 0)
    def _():
        for tsl, fr in _row_groups(bt, tl):
            h = _norm_mod(x_ref[:, tsl, :], g2_ref[...], sc_ref[...], sh_ref[...])
            h2 = h.reshape(-1, D).astype(BF16)
            h_scr[fr, :] = h2
            o_ref[:, tsl, :] = partial_ffn(h2)

    @pl.when(jnp.logical_and(f > 0, f < last))
    def _():
        o_ref[...] += partial_ffn(h_scr[...])

    @pl.when(f == last)
    def _():
        for tsl, fr in _row_groups(bt, tl):
            y = o_ref[:, tsl, :] + partial_ffn(h_scr[fr, :])
            o_ref[:, tsl, :] = x_ref[:, tsl, :] + gate_ref[...] * _rms_gain(y, g3_ref[...])


def _ffn(x, g2, sc, sh, g3, gate, wg, wu, wd, layer):
    B, L, D = x.shape
    F = wg.shape[-1]
    bt, tl = _row_blocking(B, L, FFN_ROW_TILE)
    rows = bt * tl
    tf = FFN_COL_TILE
    assert F % tf == 0 and F // tf >= 2
    row_map = lambda b, l, f: (b, l, 0)
    mod_map = lambda b, l, f: (b, 0, 0)
    vec = lambda: _resident((1, D), lambda b, l, f: (0, 0))
    vmem = 4 * rows * D * 4 + 3 * 2 * D * tf * 2 + rows * D * 2 + 3 * rows * tf * 4
    return pl.pallas_call(
        _ffn_kernel,
        grid=(B // bt, L // tl, F // tf),
        in_specs=[pl.BlockSpec((bt, tl, D), row_map), vec(),
                  pl.BlockSpec((bt, 1, D), mod_map), pl.BlockSpec((bt, 1, D), mod_map),
                  vec(), pl.BlockSpec((bt, 1, D), mod_map),
                  pl.BlockSpec((None, D, tf), lambda b, l, f: (layer, 0, f)),
                  pl.BlockSpec((None, D, tf), lambda b, l, f: (layer, 0, f)),
                  pl.BlockSpec((None, tf, D), lambda b, l, f: (layer, f, 0))],
        out_specs=pl.BlockSpec((bt, tl, D), row_map),
        out_shape=jax.ShapeDtypeStruct((B, L, D), F32),
        scratch_shapes=[pltpu.VMEM((rows, D), BF16)],
        compiler_params=_params(vmem, 3),
        name="swiglu_ffn",
    )(x, g2, sc, sh, g3, gate, wg, wu, wd)


def _log_sigmoid(x):
    return -(jnp.maximum(-x, 0.0) + jnp.log1p(jnp.exp(-jnp.abs(x))))


def _cumsum_rows(x):
    n = x.shape[0]
    row = lax.broadcasted_iota(jnp.int32, x.shape, 0)
    s = 1
    while s < n:
        x = x + jnp.where(row >= s, pltpu.roll(x, s, 0), 0.0)
        s *= 2
    return x


def _mlstm_kernel(q_ref, k_ref, v_ref, o_ref, gt_ref, ghn_ref, c0_ref, n0_ref, m0_ref,
                  x_ref, g1_ref, gate_ref, wout_ref, xo_ref, c_ref, n_ref, m_ref, a_scr,
                  *, valid_len):
    H, DK, DV = c_ref.shape
    Lc = q_ref.shape[0]
    scale = DK ** -0.5
    c = pl.program_id(1)
    scanning = c < pl.num_programs(1) - 1

    @pl.when(c == 0)
    def _():
        c_ref[...] = c0_ref[...]
        n_ref[...] = n0_ref[...]
        m_ref[...] = m0_ref[...]
        a_scr[...] = jnp.zeros_like(a_scr)

    gl = gt_ref[...]
    li_all = gl
    lf_all = _log_sigmoid(gl)
    if valid_len < Lc:
        valid = lax.broadcasted_iota(jnp.int32, gl.shape, 0) < valid_len
        li_all = jnp.where(valid, li_all, -jnp.inf)
        lf_all = jnp.where(valid, lf_all, 0.0)
    b_all = _cumsum_rows(lf_all)
    causal = (lax.broadcasted_iota(jnp.int32, (Lc, Lc), 0)
              >= lax.broadcasted_iota(jnp.int32, (Lc, Lc), 1))

    heads = range(H)
    q = [q_ref[:, h * DK:(h + 1) * DK] for h in heads]
    k = [k_ref[:, h * DK:(h + 1) * DK] for h in heads]
    v = [v_ref[:, h * DV:(h + 1) * DV] for h in heads]
    C = [c_ref[h] for h in heads]
    n = [n_ref[h] for h in heads]
    m = [m_ref[h] for h in heads]

    qk, qC, y = [], [], None
    for h in heads:
        qk.append(lax.dot_general(q[h], k[h], (((1,), (1,)), ((), ())),
                                  preferred_element_type=F32))
        qC.append(jnp.dot(q[h], C[h].astype(BF16), preferred_element_type=F32))
        yh = jnp.dot(a_scr[:, h * DV:(h + 1) * DV], wout_ref[h * DV:(h + 1) * DV, :],
                     preferred_element_type=F32)
        y = yh if y is None else y + yh
    xo_ref[...] = x_ref[...] + gate_ref[...] * _rms_gain(y, g1_ref[...])

    b, g, inter, m_t, w = [], [], [], [], []
    for h in heads:
        b.append(b_all[:, H + h:H + h + 1])
        g.append(li_all[:, h:h + 1] - b[h])
        g_row = jnp.transpose(jnp.broadcast_to(g[h], (Lc, LANES)))[0:1, :]
        dmat = jnp.where(causal, b[h] + g_row, -jnp.inf)
        inter.append(b[h] + m[h])
        m_t.append(jnp.maximum(inter[h], jnp.max(dmat, axis=-1, keepdims=True)))
        w.append(jnp.exp(dmat - m_t[h]))

    for h in heads:
        s = qk[h] * scale * w[h]
        den_s = jnp.sum(s, axis=-1, keepdims=True)
        sv = jnp.dot(s.astype(BF16), v[h], preferred_element_type=F32)
        m_new = m_t[h][Lc - 1:Lc, :]
        w_last = jnp.exp(b[h][Lc - 1:Lc, :] + g[h] - m_new)
        decay = jnp.exp(inter[h][Lc - 1:Lc, :] - m_new)
        kw = k[h].astype(F32) * w_last
        c_new = decay * C[h] + lax.dot_general(
            kw.astype(BF16), v[h], (((0,), (0,)), ((), ())), preferred_element_type=F32)
        c_ref[h] = jnp.where(scanning, c_new, C[h])
        n_ref[h] = jnp.where(scanning, decay * n[h] + jnp.sum(kw, axis=0, keepdims=True), n[h])
        m_ref[h] = jnp.where(scanning, m_new, m[h])

        inter_w = jnp.exp(inter[h] - m_t[h]) * scale
        num = sv + inter_w * qC[h]
        den = den_s + inter_w * jnp.sum(q[h].astype(F32) * n[h], axis=-1, keepdims=True)
        floor = jnp.maximum(jnp.abs(den), jnp.exp(-m_t[h]))
        hh = num * (1.0 / floor)
        hn = _rms_gain(hh, ghn_ref[:, h * DV:(h + 1) * DV])
        og = o_ref[:, h * DV:(h + 1) * DV].astype(F32)
        a_scr[:, h * DV:(h + 1) * DV] = (hn * jax.nn.sigmoid(og)).astype(BF16)


def _mlstm_mixer(z, gates, ghn, C0, layer, n0, m0, x, g1, gate, wout):
    B, L, D = x.shape
    _, _, H, DK, DV = C0.shape
    HK, HV = H * DK, H * DV
    assert HV == 2 * HK and z.shape[-1] == 2 * HK + 2 * HV
    valid_len = L
    if L >= MLSTM_CHUNK:
        Lc = MLSTM_CHUNK
        assert L % Lc == 0
    else:
        Lc = LANES
        pad = ((0, 0), (0, Lc - L), (0, 0))
        z, gates, x = jnp.pad(z, pad), jnp.pad(gates, pad), jnp.pad(x, pad)
    nc = z.shape[1] // Lc
    blk = lambda width, idx: pl.BlockSpec(
        (None, Lc, width), lambda b, c: (b, jnp.minimum(c, nc - 1), idx))
    fin = pl.BlockSpec((None, Lc, D), lambda b, c: (b, jnp.maximum(c - 1, 0), 0))
    st3 = lambda d1, d2: pl.BlockSpec((None, H, d1, d2), lambda b, c: (b, 0, 0, 0))
    vmem = (2 * Lc * (2 * HK + 2 * HV) * 2 + 2 * Lc * LANES * 4 + 4 * Lc * D * 4
            + 4 * H * DK * DV * 4 + 12 * Lc * Lc * 4 + 8 * Lc * DV * 4 + 2 * DK * DV * 4
            + HV * D * 2 + 3 * Lc * D * 4)
    xo, C, n, m = pl.pallas_call(
        functools.partial(_mlstm_kernel, valid_len=valid_len),
        grid=(B, nc + 1),
        in_specs=[blk(HK, 0), blk(HK, 1), blk(HV, 1), blk(HV, 2), blk(LANES, 0),
                  _resident((1, HV), lambda b, c: (0, 0)),
                  pl.BlockSpec((None, None, H, DK, DV), lambda b, c: (layer, b, 0, 0, 0)),
                  st3(1, DK), st3(1, 1),
                  fin, _resident((1, D), lambda b, c: (0, 0)),
                  pl.BlockSpec((None, 1, D), lambda b, c: (b, 0, 0)),
                  _resident((None, HV, D), lambda b, c: (layer, 0, 0))],
        out_specs=[fin, st3(DK, DV), st3(1, DK), st3(1, 1)],
        out_shape=[jax.ShapeDtypeStruct((B, nc * Lc, D), F32),
                   jax.ShapeDtypeStruct((B, H, DK, DV), F32),
                   jax.ShapeDtypeStruct((B, H, 1, DK), F32),
                   jax.ShapeDtypeStruct((B, H, 1, 1), F32)],
        scratch_shapes=[pltpu.VMEM((Lc, HV), BF16)],
        compiler_params=_params(vmem, 2),
        name="mlstm_mixer",
    )(z, z, z, z, gates, ghn, C0, n0.reshape(B, H, 1, DK), m0.reshape(B, H, 1, 1),
      x, g1, gate, wout)
    return xo[:, :L], C, n.reshape(B, H, DK), m.reshape(B, H)


def _conv_kernel(gb_ref, gc_ref, u_ref, w_ref, prev_ref, x_ref, g1_ref, gate_ref, wout_ref,
                 xo_ref, st_ref):
    tl, D = gc_ref.shape
    W = w_ref.shape[0]

    @pl.when(pl.program_id(1) == 0)
    def _():
        st_ref[...] = prev_ref[...]

    z = gc_ref[...].astype(F32) * u_ref[...].astype(F32)
    row = lax.broadcasted_iota(jnp.int32, (tl, D), 0)
    conv = z * w_ref[W - 1:W, :]
    for d in range(1, W):
        zd = pltpu.roll(z, d, 0)
        for r in range(d):
            zd = jnp.where(row == r, st_ref[W - 1 - d + r:W - d + r, :], zd)
        conv = conv + zd * w_ref[W - 1 - d:W - d, :]
    a = (gb_ref[...].astype(F32) * conv).astype(BF16)
    st_ref[...] = z[tl - (W - 1):, :]
    y = jnp.dot(a, wout_ref[...], preferred_element_type=F32)
    xo_ref[...] = x_ref[...] + gate_ref[...] * _rms_gain(y, g1_ref[...])


def _conv_mixer(z3, w_conv, prev, x, g1, gate, wout, layer):
    B, L, D = x.shape
    W = w_conv.shape[1]
    tl = min(L, ROW_TILE)
    assert L % tl == 0 and tl >= W - 1 and z3.shape[-1] == 3 * D
    blk = lambda idx: pl.BlockSpec((None, tl, D), lambda b, l: (b, l, idx))
    st = pl.BlockSpec((None, W - 1, D), lambda b, l: (b, 0, 0))
    vmem = 2 * 3 * tl * D * 2 + 4 * tl * D * 4 + D * D * 2 + 6 * tl * D * 4
    return pl.pallas_call(
        _conv_kernel,
        grid=(B, L // tl),
        in_specs=[blk(0), blk(1), blk(2), _resident((W, D), lambda b, l: (0, 0)), st,
                  blk(0), _resident((1, D), lambda b, l: (0, 0)),
                  pl.BlockSpec((None, 1, D), lambda b, l: (b, 0, 0)),
                  _resident((None, D, D), lambda b, l: (layer, 0, 0))],
        out_specs=[blk(0), st],
        out_shape=[jax.ShapeDtypeStruct((B, L, D), F32),
                   jax.ShapeDtypeStruct((B, W - 1, D), prev.dtype)],
        compiler_params=_params(vmem, 2),
        name="conv_mixer",
    )(z3, z3, z3, jnp.transpose(w_conv), prev, x, g1, gate, wout)


def _s5_disc_kernel(ar_ref, ai_ref, ldt_ref, br_ref, bi_ref, cr_ref, ci_ref, tn_ref, tp_ref,
                    abr_ref, abi_ref, bm_ref, cm_ref):
    G, P, N = br_ref.shape
    KT, KW, SW2 = bm_ref.shape
    GP, SW = KW // P, SW2 // 2
    dt = jnp.exp(ldt_ref[...])
    lr, lim = ar_ref[...], ai_ref[...]
    mag = jnp.exp(lr * dt)
    ab_re, ab_im = mag * jnp.cos(lim * dt), mag * jnp.sin(lim * dt)
    den = lr * lr + lim * lim
    nr = ab_re - 1.0
    fr = (nr * lr + ab_im * lim) / den
    fi = (ab_im * lr - nr * lim) / den
    abr_ref[...] = ab_re
    abi_ref[...] = ab_im
    br, bi = br_ref[...], bi_ref[...]
    bb = (fr[:, None, :] * br - fi[:, None, :] * bi, fr[:, None, :] * bi + fi[:, None, :] * br)
    cc = (cr_ref[...], -ci_ref[...])

    b_same = (lax.broadcasted_iota(jnp.int32, (KW, SW), 0) // P
              == lax.broadcasted_iota(jnp.int32, (KW, SW), 1) // N)
    c_same = (lax.broadcasted_iota(jnp.int32, (SW, KW), 0) // N
              == lax.broadcasted_iota(jnp.int32, (SW, KW), 1) // P)
    for kt in range(KT):
        for ri in range(2):
            bk = bb[ri][kt * GP:(kt + 1) * GP].reshape(KW, N).astype(BF16)
            spread = jnp.dot(bk, tn_ref[...], preferred_element_type=F32)
            bm_ref[kt, :, ri * SW:(ri + 1) * SW] = jnp.where(b_same, spread, 0.0).astype(BF16)
            ck = cc[ri][kt * GP:(kt + 1) * GP].reshape(SW, P).astype(BF16)
            spread = jnp.dot(ck, tp_ref[...], preferred_element_type=F32)
            cm_ref[kt, ri * SW:(ri + 1) * SW, :] = jnp.where(c_same, spread, 0.0).astype(BF16)


def _s5_kernel(x_ref, g_ref, sc_ref, sh_ref, bm_ref, cm_ref, ar_ref, ai_ref, dsk_ref,
               s0r_ref, s0i_ref, g1_ref, gate_ref, wa_ref, wb_ref,
               xo_ref, sr_ref, si_ref, h_scr, bu_scr, y_scr):
    B, T, D = x_ref.shape
    KT, KW, SW2 = bm_ref.shape
    SW = SW2 // 2
    rows = T * B
    LG = S5_LANE_GROUP * LANES

    @pl.when(pl.program_id(0) == 0)
    def _():
        sr_ref[...] = s0r_ref[...]
        si_ref[...] = s0i_ref[...]

    xt = pltpu.einshape("btd->tbd", x_ref[...])
    h_scr[...] = _norm_mod(xt, g_ref[...], sc_ref[...], sh_ref[...]).reshape(rows, D)

    for kt in range(KT):
        cols = slice(kt * KW, (kt + 1) * KW)
        bu = bu_scr.at[kt % 2]
        hk = h_scr[:, cols]
        bu[...] = jnp.dot(hk.astype(BF16), bm_ref[kt], preferred_element_type=F32)
        for lg in range(SW // LG):
            re_cols = slice(lg * LG, (lg + 1) * LG)
            im_cols = slice(SW + lg * LG, SW + (lg + 1) * LG)
            a_r = jnp.broadcast_to(ar_ref[kt, :, re_cols], (B, LG))
            a_i = jnp.broadcast_to(ai_ref[kt, :, re_cols], (B, LG))
            xr, xi = sr_ref[kt, :, re_cols], si_ref[kt, :, re_cols]
            for t in range(T):
                r = slice(t * B, (t + 1) * B)
                xr, xi = (a_r * xr - a_i * xi + bu[r, re_cols],
                          a_r * xi + a_i * xr + bu[r, im_cols])
                bu[r, re_cols] = xr
                bu[r, im_cols] = xi
            sr_ref[kt, :, re_cols] = xr
            si_ref[kt, :, re_cols] = xi
        yk = jnp.dot(bu[...].astype(BF16), cm_ref[kt], preferred_element_type=F32)
        yk = yk + dsk_ref[:, cols] * hk
        y_scr[:, cols] = jax.nn.gelu(yk).astype(y_scr.dtype)

    yg = y_scr[...]
    z = (jnp.dot(yg, wa_ref[...], preferred_element_type=F32)
         * jax.nn.sigmoid(jnp.dot(yg, wb_ref[...], preferred_element_type=F32)))
    out = xt + gate_ref[...] * _rms_gain(z, g1_ref[...]).reshape(T, B, D)
    xo_ref[...] = pltpu.einshape("tbd->btd", out)


def _s5_mixer(x, g, sc, sh, bmat, cmat, a_r, a_i, dsk, s0r, s0i, g1, gate, wout, layer):
    B, L, D = x.shape
    KT, KW, SW2 = bmat.shape
    SW = SW2 // 2
    assert B == SUBLANES and SW % (S5_LANE_GROUP * LANES) == 0
    T = min(S5_CHUNK, L)
    assert L % T == 0
    rows = T * B
    c0 = lambda *shape: _resident(shape, lambda c: (0,) * len(shape))
    vmem = (4 * rows * D * 4 + 2 * KT * KW * SW2 * 2 + 2 * D * D * 2 + rows * D * 4
            + 2 * rows * SW2 * 4 + rows * D * 2 + 5 * rows * D * 4 + 6 * KT * B * SW * 4)
    return pl.pallas_call(
        _s5_kernel,
        grid=(L // T,),
        in_specs=[pl.BlockSpec((B, T, D), lambda c: (0, c, 0)), c0(1, D), c0(B, D), c0(B, D),
                  c0(KT, KW, SW2), c0(KT, SW2, KW), c0(KT, 1, SW), c0(KT, 1, SW), c0(1, D),
                  c0(KT, B, SW), c0(KT, B, SW), c0(1, D), c0(B, D),
                  _resident((None, D, D), lambda c: (layer, 0, 0)),
                  _resident((None, D, D), lambda c: (layer, 0, 1))],
        out_specs=[pl.BlockSpec((B, T, D), lambda c: (0, c, 0)),
                   pl.BlockSpec((KT, B, SW), lambda c: (0, 0, 0)),
                   pl.BlockSpec((KT, B, SW), lambda c: (0, 0, 0))],
        out_shape=[jax.ShapeDtypeStruct((B, L, D), F32),
                   jax.ShapeDtypeStruct((KT, B, SW), F32),
                   jax.ShapeDtypeStruct((KT, B, SW), F32)],
        scratch_shapes=[pltpu.VMEM((rows, D), F32), pltpu.VMEM((2, rows, SW2), F32),
                        pltpu.VMEM((rows, D), BF16)],
        compiler_params=_params(vmem, 1),
        name="s5_mixer",
    )(x, g, sc, sh, bmat, cmat, a_r, a_i, dsk, s0r, s0i, g1, gate, wout, wout)


def _s5_weights(a_re, a_im, b_re, b_im, c_re, c_im, d_skip, log_dt):
    G, N, P = b_re.shape
    KW = MXU_DIM_V7X
    GP = KW // P
    KT, SW = G // GP, GP * N
    full = lambda *shape: pl.BlockSpec(shape, lambda: (0,) * len(shape))
    tile_n = jnp.tile(jnp.eye(N, dtype=BF16), (1, GP))
    tile_p = jnp.tile(jnp.eye(P, dtype=BF16), (1, GP))
    ab_re, ab_im, bmat, cmat = pl.pallas_call(
        _s5_disc_kernel,
        in_specs=[full(G, N), full(G, N), full(G, 1), full(G, P, N), full(G, P, N),
                  full(G, N, P), full(G, N, P), full(N, SW), full(P, KW)],
        out_specs=[full(G, N), full(G, N), full(KT, KW, 2 * SW), full(KT, 2 * SW, KW)],
        out_shape=[jax.ShapeDtypeStruct((G, N), F32), jax.ShapeDtypeStruct((G, N), F32),
                   jax.ShapeDtypeStruct((KT, KW, 2 * SW), BF16),
                   jax.ShapeDtypeStruct((KT, 2 * SW, KW), BF16)],
        compiler_params=pltpu.CompilerParams(vmem_limit_bytes=VMEM_CAP),
        name="s5_discretize",
    )(a_re, a_im, log_dt.reshape(G, 1), jnp.swapaxes(b_re, 1, 2), jnp.swapaxes(b_im, 1, 2),
      jnp.swapaxes(c_re, 1, 2), jnp.swapaxes(c_im, 1, 2), tile_n, tile_p)
    return (bmat, cmat, ab_re.reshape(KT, 1, SW), ab_im.reshape(KT, 1, SW),
            d_skip.reshape(1, G * P))


def _trunk(x, mod, row0, st_C, st_n, st_m, st_conv, st_re, st_im, p):
    B, L, D = x.shape
    depth = p['g_norm'].shape[0]
    new_C, new_n, new_m, new_conv, new_re, new_im = [], [], [], [], [], []
    for i in range(depth):
        m6 = mod[i, row0:row0 + B].reshape(B, 6, 1, D)
        sh1, sc1, g1, sh2, sc2, g2 = (m6[:, j] for j in range(6))
        gn = p['g_norm'][i]
        gvec = lambda r: gn[r].reshape(1, D)
        kind, j = i % N_MIXERS, i // N_MIXERS
        if kind == 0:
            H = st_C.shape[2]
            nz = p['wA_in'].shape[-1] - 2 * H
            w_gates = jnp.pad(p['wA_in'][j, :, nz:], ((0, 0), (0, LANES - 2 * H)))
            b_gates = jnp.pad(p['bA_gates'][j], (0, LANES - 2 * H)).reshape(1, LANES)
            z, gates = _inproj(x, gvec(0), sc1, sh1, p['wA_in'], j, nz, w_gates, b_gates)
            x, C, n, m = _mlstm_mixer(z, gates, p['gA_hnorm'][j].reshape(1, -1), st_C, j,
                                      st_n[j], st_m[j], x, gvec(1), g1, p['wA_out'])
            new_C.append(C); new_n.append(n); new_m.append(m)
        elif kind == 1:
            z3 = _inproj(x, gvec(0), sc1, sh1, p['wB_in'], j, p['wB_in'].shape[-1])
            x, cv = _conv_mixer(z3, p['wB_conv'][j], st_conv[j], x, gvec(1), g1,
                                p['wB_out'], j)
            new_conv.append(cv)
        else:
            bmat, cmat, a_r, a_i, dsk = _s5_weights(
                p['s5_A_re'][j], p['s5_A_im'][j], p['s5_B_re'][j], p['s5_B_im'][j],
                p['s5_C_re'][j], p['s5_C_im'][j], p['s5_D'][j], p['s5_log_dt'][j])
            KT, _, SW2 = bmat.shape
            to_lanes = lambda s: jnp.swapaxes(s.reshape(B, KT, SW2 // 2), 0, 1)
            x, sr, si = _s5_mixer(x, gvec(0), sc1.reshape(B, D), sh1.reshape(B, D), bmat, cmat,
                                  a_r, a_i, dsk, to_lanes(st_re[j]), to_lanes(st_im[j]),
                                  gvec(1), g1.reshape(B, D), p['wC_out'], j)
            from_lanes = lambda s: jnp.swapaxes(s, 0, 1).reshape(st_re[j].shape)
            new_re.append(from_lanes(sr)); new_im.append(from_lanes(si))
        x = _ffn(x, gvec(2), sc2, sh2, gvec(3), g2, p['w_ffn_gate'], p['w_ffn_up'],
                 p['w_ffn_down'], i)
    return (x, jnp.stack(new_C), jnp.stack(new_n), jnp.stack(new_m), jnp.stack(new_conv),
            jnp.stack(new_re), jnp.stack(new_im))


def kernel(x_prompt, x_sample, state_mlstm_C, state_mlstm_n, state_mlstm_m, state_conv,
           state_s5_re, state_s5_im, c_prompt, c_sample, w_mod, b_mod, g_norm, wA_in,
           bA_gates, gA_hnorm, wA_out, wB_in, wB_conv, wB_out, s5_A_re, s5_A_im, s5_B_re,
           s5_B_im, s5_C_re, s5_C_im, s5_D, s5_log_dt, wC_out, w_ffn_gate, w_ffn_up,
           w_ffn_down):
    cast = lambda w: w.astype(BF16)
    p = dict(g_norm=g_norm, wA_in=cast(wA_in), bA_gates=bA_gates, gA_hnorm=gA_hnorm,
             wA_out=cast(wA_out), wB_in=cast(wB_in), wB_conv=wB_conv, wB_out=cast(wB_out),
             s5_A_re=s5_A_re, s5_A_im=s5_A_im, s5_B_re=s5_B_re, s5_B_im=s5_B_im,
             s5_C_re=s5_C_re, s5_C_im=s5_C_im, s5_D=s5_D, s5_log_dt=s5_log_dt,
             wC_out=cast(wC_out), w_ffn_gate=cast(w_ffn_gate), w_ffn_up=cast(w_ffn_up),
             w_ffn_down=cast(w_ffn_down))
    bp = x_prompt.shape[0]
    mod = _modulation(jnp.concatenate([c_prompt, c_sample], axis=0), w_mod, b_mod)
    zeros = lambda s: jnp.zeros((s.shape[0], bp) + s.shape[2:], s.dtype)
    outs_p = _trunk(x_prompt, mod, 0, zeros(state_mlstm_C), zeros(state_mlstm_n),
                    zeros(state_mlstm_m), zeros(state_conv), zeros(state_s5_re),
                    zeros(state_s5_im), p)
    outs_s = _trunk(x_sample, mod, bp, state_mlstm_C, state_mlstm_n, state_mlstm_m,
                    state_conv, state_s5_re, state_s5_im, p)
    return (outs_p[0], outs_s[0]) + tuple(outs_p[1:]) + tuple(outs_s[1:])
```

```python
import functools

import jax
import jax.numpy as jnp
from jax import lax
from jax.experimental import pallas as pl
from jax.experimental.pallas import tpu as pltpu

F32 = jnp.float32
BF16 = jnp.bfloat16
EPS = 1e-6
N_MIXERS = 3

LANES = 128
SUBLANES = 8
MXU_DIM_V7X = 256
VMEM_BYTES_V7X = 64 * 1024 * 1024
VMEM_CAP = VMEM_BYTES_V7X - 6 * 1024 * 1024

ROW_TILE = 512
ROW_GROUP = MXU_DIM_V7X
FFN_ROW_TILE = 1024
INPROJ_ROW_TILE = 1024
INPROJ_COL_TILE = 2048
FFN_COL_TILE = 512
MOD_COL_TILE = 1024
MLSTM_CHUNK = 256
S5_CHUNK = 32
S5_LANE_GROUP = 4


def _params(vmem_bytes, n_grid):
    limit = int(min(VMEM_CAP, max(vmem_bytes * 5 // 4 + (4 << 20), 16 << 20)))
    return pltpu.CompilerParams(dimension_semantics=("arbitrary",) * n_grid,
                                vmem_limit_bytes=limit)


def _resident(block_shape, index_map):
    return pl.BlockSpec(block_shape, index_map, pipeline_mode=pl.Buffered(1))


def _row_blocking(B, L, tile=ROW_TILE):
    if L >= ROW_TILE:
        tile = min(tile, L)
        assert L % tile == 0
        return 1, tile
    assert L % SUBLANES == 0
    return B, L


def _norm_mod(x, g, sc, sh):
    ms = jnp.mean(x * x, axis=-1, keepdims=True)
    y = x * lax.rsqrt(ms + EPS) * g
    return y * (1.0 + sc) + sh


def _rms_gain(y, g):
    ms = jnp.mean(y * y, axis=-1, keepdims=True)
    return y * lax.rsqrt(ms + EPS) * g


def _row_groups(bt, tl):
    if bt != 1 or tl <= ROW_GROUP:
        return [(slice(0, tl), slice(0, bt * tl))]
    assert tl % ROW_GROUP == 0
    return [(slice(q * ROW_GROUP, (q + 1) * ROW_GROUP),) * 2 for q in range(tl // ROW_GROUP)]


def _mod_kernel(c_ref, w_ref, b_ref, o_ref):
    c = c_ref[...]
    sc = (c * jax.nn.sigmoid(c)).astype(BF16)
    o_ref[...] = jnp.dot(sc, w_ref[...].astype(BF16), preferred_element_type=F32) + b_ref[...]


def _modulation(c_all, w_mod, b_mod):
    depth, D, N = w_mod.shape
    R = c_all.shape[0]
    tn = min(MOD_COL_TILE, N)
    assert N % tn == 0
    vmem = 2 * D * tn * 4 + 2 * R * tn * 4 + R * D * 4
    return pl.pallas_call(
        _mod_kernel,
        grid=(depth, N // tn),
        in_specs=[_resident((R, D), lambda i, j: (0, 0)),
                  pl.BlockSpec((None, D, tn), lambda i, j: (i, 0, j)),
                  pl.BlockSpec((None, 1, tn), lambda i, j: (i, 0, j))],
        out_specs=pl.BlockSpec((None, R, tn), lambda i, j: (i, 0, j)),
        out_shape=jax.ShapeDtypeStruct((depth, R, N), F32),
        compiler_params=_params(vmem, 2),
        name="adaln_modulation",
    )(c_all, w_mod, b_mod.reshape(depth, 1, N))


def _inproj_kernel(x_ref, g_ref, sc_ref, sh_ref, w_ref, *rest, with_gates):
    if with_gates:
        wg_ref, bg_ref, z_ref, gates_ref, h_scr = rest
    else:
        z_ref, h_scr = rest
    bt, tl, D = x_ref.shape
    j = pl.program_id(2)

    @pl.when(j == 0)
    def _():
        for tsl, fr in _row_groups(bt, tl):
            h = _norm_mod(x_ref[:, tsl, :], g_ref[...], sc_ref[...], sh_ref[...])
            h2 = h.reshape(-1, D).astype(BF16)
            h_scr[fr, :] = h2
            z = jnp.dot(h2, w_ref[...], preferred_element_type=F32)
            z_ref[:, tsl, :] = z.reshape(bt, -1, z.shape[-1]).astype(z_ref.dtype)
            if with_gates:
                gates = jnp.dot(h2, wg_ref[...], preferred_element_type=F32) + bg_ref[...]
                gates_ref[:, tsl, :] = gates.reshape(bt, -1, gates.shape[-1])

    @pl.when(j > 0)
    def _():
        z = jnp.dot(h_scr[...], w_ref[...], preferred_element_type=F32)
        z_ref[...] = z.reshape(z_ref.shape).astype(z_ref.dtype)


def _inproj(x, g, sc, sh, w, layer, N, w_gates=None, b_gates=None):
    B, L, D = x.shape
    bt, tl = _row_blocking(B, L, INPROJ_ROW_TILE)
    rows = bt * tl
    tn = INPROJ_COL_TILE
    while N % tn:
        tn //= 2
    assert tn % LANES == 0
    with_gates = w_gates is not None
    row_map = lambda b, l, j: (b, l, 0)
    mod_map = lambda b, l, j: (b, 0, 0)
    in_specs = [pl.BlockSpec((bt, tl, D), row_map),
                _resident((1, D), lambda b, l, j: (0, 0)),
                pl.BlockSpec((bt, 1, D), mod_map),
                pl.BlockSpec((bt, 1, D), mod_map),
                pl.BlockSpec((None, D, tn), lambda b, l, j: (layer, 0, j))]
    args = [x, g, sc, sh, w]
    out_specs = [pl.BlockSpec((bt, tl, tn), lambda b, l, j: (b, l, j))]
    out_shape = [jax.ShapeDtypeStruct((B, L, N), BF16)]
    vmem = 2 * rows * D * 4 + 2 * D * tn * 2 + 2 * rows * tn * 2 + rows * D * 2
    if with_gates:
        in_specs += [_resident((D, LANES), lambda b, l, j: (0, 0)),
                     _resident((1, LANES), lambda b, l, j: (0, 0))]
        args += [w_gates, b_gates]
        out_specs.append(pl.BlockSpec((bt, tl, LANES), row_map))
        out_shape.append(jax.ShapeDtypeStruct((B, L, LANES), F32))
        vmem += D * LANES * 2 + 2 * rows * LANES * 4
    out = pl.pallas_call(
        functools.partial(_inproj_kernel, with_gates=with_gates),
        grid=(B // bt, L // tl, N // tn),
        in_specs=in_specs,
        out_specs=out_specs,
        out_shape=out_shape,
        scratch_shapes=[pltpu.VMEM((rows, D), BF16)],
        compiler_params=_params(vmem, 3),
        name="norm_mod_inproj",
    )(*args)
    return out if with_gates else out[0]


def _ffn_kernel(x_ref, g2_ref, sc_ref, sh_ref, g3_ref, gate_ref, wg_ref, wu_ref, wd_ref,
                o_ref, h_scr):
    bt, tl, D = x_ref.shape
    f = pl.program_id(2)
    last = pl.num_programs(2) - 1

    def partial_ffn(h2):
        gg = jnp.dot(h2, wg_ref[...], preferred_element_type=F32)
        uu = jnp.dot(h2, wu_ref[...], preferred_element_type=F32)
        act = (gg * jax.nn.sigmoid(gg) * uu).astype(BF16)
        return jnp.dot(act, wd_ref[...], preferred_element_type=F32).reshape(bt, -1, D)

    @pl.when(f == 0)
    def _():
        for tsl, fr in _row_groups(bt, tl):
            h = _norm_mod(x_ref[:, tsl, :], g2_ref[...], sc_ref[...], sh_ref[...])
            h2 = h.reshape(-1, D).astype(BF16)
            h_scr[fr, :] = h2
            o_ref[:, tsl, :] = partial_ffn(h2)

    @pl.when(jnp.logical_and(f > 0, f < last))
    def _():
        o_ref[...] += partial_ffn(h_scr[...])

    @pl.when(f == last)
    def _():
        for tsl, fr in _row_groups(bt, tl):
            y = o_ref[:, tsl, :] + partial_ffn(h_scr[fr, :])
            o_ref[:, tsl, :] = x_ref[:, tsl, :] + gate_ref[...] * _rms_gain(y, g3_ref[...])


def _ffn(x, g2, sc, sh, g3, gate, wg, wu, wd, layer):
    B, L, D = x.shape
    F = wg.shape[-1]
    bt, tl = _row_blocking(B, L, FFN_ROW_TILE)
    rows = bt * tl
    tf = FFN_COL_TILE
    assert F % tf == 0 and F // tf >= 2
    row_map = lambda b, l, f: (b, l, 0)
    mod_map = lambda b, l, f: (b, 0, 0)
    vec = lambda: _resident((1, D), lambda b, l, f: (0, 0))
    vmem = 4 * rows * D * 4 + 3 * 2 * D * tf * 2 + rows * D * 2 + 3 * rows * tf * 4
    return pl.pallas_call(
        _ffn_kernel,
        grid=(B // bt, L // tl, F // tf),
        in_specs=[pl.BlockSpec((bt, tl, D), row_map), vec(),
                  pl.BlockSpec((bt, 1, D), mod_map), pl.BlockSpec((bt, 1, D), mod_map),
                  vec(), pl.BlockSpec((bt, 1, D), mod_map),
                  pl.BlockSpec((None, D, tf), lambda b, l, f: (layer, 0, f)),
                  pl.BlockSpec((None, D, tf), lambda b, l, f: (layer, 0, f)),
                  pl.BlockSpec((None, tf, D), lambda b, l, f: (layer, f, 0))],
        out_specs=pl.BlockSpec((bt, tl, D), row_map),
        out_shape=jax.ShapeDtypeStruct((B, L, D), F32),
        scratch_shapes=[pltpu.VMEM((rows, D), BF16)],
        compiler_params=_params(vmem, 3),
        name="swiglu_ffn",
    )(x, g2, sc, sh, g3, gate, wg, wu, wd)


def _log_sigmoid(x):
    return -(jnp.maximum(-x, 0.0) + jnp.log1p(jnp.exp(-jnp.abs(x))))


def _cumsum_rows(x):
    n = x.shape[0]
    row = lax.broadcasted_iota(jnp.int32, x.shape, 0)
    s = 1
    while s < n:
        x = x + jnp.where(row >= s, pltpu.roll(x, s, 0), 0.0)
        s *= 2
    return x


def _mlstm_kernel(q_ref, k_ref, v_ref, o_ref, gt_ref, ghn_ref, c0_ref, n0_ref, m0_ref,
                  x_ref, g1_ref, gate_ref, wout_ref, xo_ref, c_ref, n_ref, m_ref, a_scr,
                  *, valid_len):
    H, DK, DV = c_ref.shape
    Lc = q_ref.shape[0]
    scale = DK ** -0.5
    c = pl.program_id(1)
    scanning = c < pl.num_programs(1) - 1

    @pl.when(c == 0)
    def _():
        c_ref[...] = c0_ref[...]
        n_ref[...] = n0_ref[...]
        m_ref[...] = m0_ref[...]
        a_scr[...] = jnp.zeros_like(a_scr)

    gl = gt_ref[...]
    li_all = gl
    lf_all = _log_sigmoid(gl)
    if valid_len < Lc:
        valid = lax.broadcasted_iota(jnp.int32, gl.shape, 0) < valid_len
        li_all = jnp.where(valid, li_all, -jnp.inf)
        lf_all = jnp.where(valid, lf_all, 0.0)
    b_all = _cumsum_rows(lf_all)
    causal = (lax.broadcasted_iota(jnp.int32, (Lc, Lc), 0)
              >= lax.broadcasted_iota(jnp.int32, (Lc, Lc), 1))

    heads = range(H)
    q = [q_ref[:, h * DK:(h + 1) * DK] for h in heads]
    k = [k_ref[:, h * DK:(h + 1) * DK] for h in heads]
    v = [v_ref[:, h * DV:(h + 1) * DV] for h in heads]
    C = [c_ref[h] for h in heads]
    n = [n_ref[h] for h in heads]
    m = [m_ref[h] for h in heads]

    qk, qC, y = [], [], None
    for h in heads:
        qk.append(lax.dot_general(q[h], k[h], (((1,), (1,)), ((), ())),
                                  preferred_element_type=F32))
        qC.append(jnp.dot(q[h], C[h].astype(BF16), preferred_element_type=F32))
        yh = jnp.dot(a_scr[:, h * DV:(h + 1) * DV], wout_ref[h * DV:(h + 1) * DV, :],
                     preferred_element_type=F32)
        y = yh if y is None else y + yh
    xo_ref[...] = x_ref[...] + gate_ref[...] * _rms_gain(y, g1_ref[...])

    b, g, inter, m_t, w = [], [], [], [], []
    for h in heads:
        b.append(b_all[:, H + h:H + h + 1])
        g.append(li_all[:, h:h + 1] - b[h])
        g_row = jnp.transpose(jnp.broadcast_to(g[h], (Lc, LANES)))[0:1, :]
        dmat = jnp.where(causal, b[h] + g_row, -jnp.inf)
        inter.append(b[h] + m[h])
        m_t.append(jnp.maximum(inter[h], jnp.max(dmat, axis=-1, keepdims=True)))
        w.append(jnp.exp(dmat - m_t[h]))

    for h in heads:
        s = qk[h] * scale * w[h]
        den_s = jnp.sum(s, axis=-1, keepdims=True)
        sv = jnp.dot(s.astype(BF16), v[h], preferred_element_type=F32)
        m_new = m_t[h][Lc - 1:Lc, :]
        w_last = jnp.exp(b[h][Lc - 1:Lc, :] + g[h] - m_new)
        decay = jnp.exp(inter[h][Lc - 1:Lc, :] - m_new)
        kw = k[h].astype(F32) * w_last
        c_new = decay * C[h] + lax.dot_general(
            kw.astype(BF16), v[h], (((0,), (0,)), ((), ())), preferred_element_type=F32)
        c_ref[h] = jnp.where(scanning, c_new, C[h])
        n_ref[h] = jnp.where(scanning, decay * n[h] + jnp.sum(kw, axis=0, keepdims=True), n[h])
        m_ref[h] = jnp.where(scanning, m_new, m[h])

        inter_w = jnp.exp(inter[h] - m_t[h]) * scale
        num = sv + inter_w * qC[h]
        den = den_s + inter_w * jnp.sum(q[h].astype(F32) * n[h], axis=-1, keepdims=True)
        floor = jnp.maximum(jnp.abs(den), jnp.exp(-m_t[h]))
        hh = num * (1.0 / floor)
        hn = _rms_gain(hh, ghn_ref[:, h * DV:(h + 1) * DV])
        og = o_ref[:, h * DV:(h + 1) * DV].astype(F32)
        a_scr[:, h * DV:(h + 1) * DV] = (hn * jax.nn.sigmoid(og)).astype(BF16)


def _mlstm_mixer(z, gates, ghn, C0, layer, n0, m0, x, g1, gate, wout):
    B, L, D = x.shape
    _, _, H, DK, DV = C0.shape
    HK, HV = H * DK, H * DV
    assert HV == 2 * HK and z.shape[-1] == 2 * HK + 2 * HV
    valid_len = L
    if L >= MLSTM_CHUNK:
        Lc = MLSTM_CHUNK
        assert L % Lc == 0
    else:
        Lc = LANES
        pad = ((0, 0), (0, Lc - L), (0, 0))
        z, gates, x = jnp.pad(z, pad), jnp.pad(gates, pad), jnp.pad(x, pad)
    nc = z.shape[1] // Lc
    blk = lambda width, idx: pl.BlockSpec(
        (None, Lc, width), lambda b, c: (b, jnp.minimum(c, nc - 1), idx))
    fin = pl.BlockSpec((None, Lc, D), lambda b, c: (b, jnp.maximum(c - 1, 0), 0))
    st3 = lambda d1, d2: pl.BlockSpec((None, H, d1, d2), lambda b, c: (b, 0, 0, 0))
    vmem = (2 * Lc * (2 * HK + 2 * HV) * 2 + 2 * Lc * LANES * 4 + 4 * Lc * D * 4
            + 4 * H * DK * DV * 4 + 12 * Lc * Lc * 4 + 8 * Lc * DV * 4 + 2 * DK * DV * 4
            + HV * D * 2 + 3 * Lc * D * 4)
    xo, C, n, m = pl.pallas_call(
        functools.partial(_mlstm_kernel, valid_len=valid_len),
        grid=(B, nc + 1),
        in_specs=[blk(HK, 0), blk(HK, 1), blk(HV, 1), blk(HV, 2), blk(LANES, 0),
                  _resident((1, HV), lambda b, c: (0, 0)),
                  pl.BlockSpec((None, None, H, DK, DV), lambda b, c: (layer, b, 0, 0, 0)),
                  st3(1, DK), st3(1, 1),
                  fin, _resident((1, D), lambda b, c: (0, 0)),
                  pl.BlockSpec((None, 1, D), lambda b, c: (b, 0, 0)),
                  _resident((None, HV, D), lambda b, c: (layer, 0, 0))],
        out_specs=[fin, st3(DK, DV), st3(1, DK), st3(1, 1)],
        out_shape=[jax.ShapeDtypeStruct((B, nc * Lc, D), F32),
                   jax.ShapeDtypeStruct((B, H, DK, DV), F32),
                   jax.ShapeDtypeStruct((B, H, 1, DK), F32),
                   jax.ShapeDtypeStruct((B, H, 1, 1), F32)],
        scratch_shapes=[pltpu.VMEM((Lc, HV), BF16)],
        compiler_params=_params(vmem, 2),
        name="mlstm_mixer",
    )(z, z, z, z, gates, ghn, C0, n0.reshape(B, H, 1, DK), m0.reshape(B, H, 1, 1),
      x, g1, gate, wout)
    return xo[:, :L], C, n.reshape(B, H, DK), m.reshape(B, H)


def _conv_kernel(gb_ref, gc_ref, u_ref, w_ref, prev_ref, x_ref, g1_ref, gate_ref, wout_ref,
                 xo_ref, st_ref):
    tl, D = gc_ref.shape
    W = w_ref.shape[0]

    @pl.when(pl.program_id(1) == 0)
    def _():
        st_ref[...] = prev_ref[...]

    z = gc_ref[...].astype(F32) * u_ref[...].astype(F32)
    row = lax.broadcasted_iota(jnp.int32, (tl, D), 0)
    conv = z * w_ref[W - 1:W, :]
    for d in range(1, W):
        zd = pltpu.roll(z, d, 0)
        for r in range(d):
            zd = jnp.where(row == r, st_ref[W - 1 - d + r:W - d + r, :], zd)
        conv = conv + zd * w_ref[W - 1 - d:W - d, :]
    a = (gb_ref[...].astype(F32) * conv).astype(BF16)
    st_ref[...] = z[tl - (W - 1):, :]
    y = jnp.dot(a, wout_ref[...], preferred_element_type=F32)
    xo_ref[...] = x_ref[...] + gate_ref[...] * _rms_gain(y, g1_ref[...])


def _conv_mixer(z3, w_conv, prev, x, g1, gate, wout, layer):
    B, L, D = x.shape
    W = w_conv.shape[1]
    tl = min(L, ROW_TILE)
    assert L % tl == 0 and tl >= W - 1 and z3.shape[-1] == 3 * D
    blk = lambda idx: pl.BlockSpec((None, tl, D), lambda b, l: (b, l, idx))
    st = pl.BlockSpec((None, W - 1, D), lambda b, l: (b, 0, 0))
    vmem = 2 * 3 * tl * D * 2 + 4 * tl * D * 4 + D * D * 2 + 6 * tl * D * 4
    return pl.pallas_call(
        _conv_kernel,
        grid=(B, L // tl),
        in_specs=[blk(0), blk(1), blk(2), _resident((W, D), lambda b, l: (0, 0)), st,
                  blk(0), _resident((1, D), lambda b, l: (0, 0)),
                  pl.BlockSpec((None, 1, D), lambda b, l: (b, 0, 0)),
                  _resident((None, D, D), lambda b, l: (layer, 0, 0))],
        out_specs=[blk(0), st],
        out_shape=[jax.ShapeDtypeStruct((B, L, D), F32),
                   jax.ShapeDtypeStruct((B, W - 1, D), prev.dtype)],
        compiler_params=_params(vmem, 2),
        name="conv_mixer",
    )(z3, z3, z3, jnp.transpose(w_conv), prev, x, g1, gate, wout)


def _s5_disc_kernel(ar_ref, ai_ref, ldt_ref, br_ref, bi_ref, cr_ref, ci_ref, tn_ref, tp_ref,
                    abr_ref, abi_ref, bm_ref, cm_ref):
    G, P, N = br_ref.shape
    KT, KW, SW2 = bm_ref.shape
    GP, SW = KW // P, SW2 // 2
    dt = jnp.exp(ldt_ref[...])
    lr, lim = ar_ref[...], ai_ref[...]
    mag = jnp.exp(lr * dt)
    ab_re, ab_im = mag * jnp.cos(lim * dt), mag * jnp.sin(lim * dt)
    den = lr * lr + lim * lim
    nr = ab_re - 1.0
    fr = (nr * lr + ab_im * lim) / den
    fi = (ab_im * lr - nr * lim) / den
    abr_ref[...] = ab_re
    abi_ref[...] = ab_im
    br, bi = br_ref[...], bi_ref[...]
    bb = (fr[:, None, :] * br - fi[:, None, :] * bi, fr[:, None, :] * bi + fi[:, None, :] * br)
    cc = (cr_ref[...], -ci_ref[...])

    b_same = (lax.broadcasted_iota(jnp.int32, (KW, SW), 0) // P
              == lax.broadcasted_iota(jnp.int32, (KW, SW), 1) // N)
    c_same = (lax.broadcasted_iota(jnp.int32, (SW, KW), 0) // N
              == lax.broadcasted_iota(jnp.int32, (SW, KW), 1) // P)
    for kt in range(KT):
        for ri in range(2):
            bk = bb[ri][kt * GP:(kt + 1) * GP].reshape(KW, N).astype(BF16)
            spread = jnp.dot(bk, tn_ref[...], preferred_element_type=F32)
            bm_ref[kt, :, ri * SW:(ri + 1) * SW] = jnp.where(b_same, spread, 0.0).astype(BF16)
            ck = cc[ri][kt * GP:(kt + 1) * GP].reshape(SW, P).astype(BF16)
            spread = jnp.dot(ck, tp_ref[...], preferred_element_type=F32)
            cm_ref[kt, ri * SW:(ri + 1) * SW, :] = jnp.where(c_same, spread, 0.0).astype(BF16)


def _s5_kernel(x_ref, g_ref, sc_ref, sh_ref, bm_ref, cm_ref, ar_ref, ai_ref, dsk_ref,
               s0r_ref, s0i_ref, g1_ref, gate_ref, wa_ref, wb_ref,
               xo_ref, sr_ref, si_ref, h_scr, bu_scr, y_scr, xt_scr):
    B, T, D = x_ref.shape
    KT, KW, SW2 = bm_ref.shape
    SW = SW2 // 2
    rows = T * B
    LG = S5_LANE_GROUP * LANES
    c = pl.program_id(0)
    scanning = c < pl.num_programs(0) - 1

    @pl.when(c == 0)
    def _():
        sr_ref[...] = s0r_ref[...]
        si_ref[...] = s0i_ref[...]
        y_scr[...] = jnp.zeros_like(y_scr)
        xt_scr[...] = jnp.zeros_like(xt_scr)

    yg_prev = y_scr[...]
    xt_prev = xt_scr[...]

    xt = pltpu.einshape("btd->tbd", x_ref[...])
    xt_scr[...] = xt
    h_scr[...] = _norm_mod(xt, g_ref[...], sc_ref[...], sh_ref[...]).reshape(rows, D)
    z_cols = []

    for kt in range(KT):
        cols = slice(kt * KW, (kt + 1) * KW)
        bu = bu_scr.at[kt % 2]
        hk = h_scr[:, cols]
        bu[...] = jnp.dot(hk.astype(BF16), bm_ref[kt], preferred_element_type=F32)
        z_cols.append(jnp.dot(yg_prev, wa_ref[:, cols], preferred_element_type=F32)
                      * jax.nn.sigmoid(jnp.dot(yg_prev, wb_ref[:, cols],
                                               preferred_element_type=F32)))
        for lg in range(SW // LG):
            re_cols = slice(lg * LG, (lg + 1) * LG)
            im_cols = slice(SW + lg * LG, SW + (lg + 1) * LG)
            a_r = jnp.broadcast_to(ar_ref[kt, :, re_cols], (B, LG))
            a_i = jnp.broadcast_to(ai_ref[kt, :, re_cols], (B, LG))
            xr0, xi0 = sr_ref[kt, :, re_cols], si_ref[kt, :, re_cols]
            xr, xi = xr0, xi0
            for t in range(T):
                r = slice(t * B, (t + 1) * B)
                xr, xi = (a_r * xr - a_i * xi + bu[r, re_cols],
                          a_r * xi + a_i * xr + bu[r, im_cols])
                bu[r, re_cols] = xr
                bu[r, im_cols] = xi
            sr_ref[kt, :, re_cols] = jnp.where(scanning, xr, xr0)
            si_ref[kt, :, re_cols] = jnp.where(scanning, xi, xi0)
        yk = jnp.dot(bu[...].astype(BF16), cm_ref[kt], preferred_element_type=F32)
        yk = yk + dsk_ref[:, cols] * hk
        y_scr[:, cols] = jax.nn.gelu(yk).astype(y_scr.dtype)

    z = jnp.concatenate(z_cols, axis=1)
    out = xt_prev + gate_ref[...] * _rms_gain(z, g1_ref[...]).reshape(T, B, D)
    xo_ref[...] = pltpu.einshape("tbd->btd", out)


def _s5_mixer(x, g, sc, sh, bmat, cmat, a_r, a_i, dsk, s0r, s0i, g1, gate, wout, layer):
    B, L, D = x.shape
    KT, KW, SW2 = bmat.shape
    SW = SW2 // 2
    assert B == SUBLANES and SW % (S5_LANE_GROUP * LANES) == 0
    T = min(S5_CHUNK, L)
    assert L % T == 0
    rows = T * B
    c0 = lambda *shape: _resident(shape, lambda c: (0,) * len(shape))
    vmem = (4 * rows * D * 4 + 2 * KT * KW * SW2 * 2 + 2 * D * D * 2 + rows * D * 4
            + 2 * rows * SW2 * 4 + rows * D * 2 + 5 * rows * D * 4 + 6 * KT * B * SW * 4)
    nc = L // T
    return pl.pallas_call(
        _s5_kernel,
        grid=(nc + 1,),
        in_specs=[pl.BlockSpec((B, T, D), lambda c: (0, jnp.minimum(c, nc - 1), 0)),
                  c0(1, D), c0(B, D), c0(B, D),
                  c0(KT, KW, SW2), c0(KT, SW2, KW), c0(KT, 1, SW), c0(KT, 1, SW), c0(1, D),
                  c0(KT, B, SW), c0(KT, B, SW), c0(1, D), c0(B, D),
                  _resident((None, D, D), lambda c: (layer, 0, 0)),
                  _resident((None, D, D), lambda c: (layer, 0, 1))],
        out_specs=[pl.BlockSpec((B, T, D), lambda c: (0, jnp.maximum(c - 1, 0), 0)),
                   pl.BlockSpec((KT, B, SW), lambda c: (0, 0, 0)),
                   pl.BlockSpec((KT, B, SW), lambda c: (0, 0, 0))],
        out_shape=[jax.ShapeDtypeStruct((B, L, D), F32),
                   jax.ShapeDtypeStruct((KT, B, SW), F32),
                   jax.ShapeDtypeStruct((KT, B, SW), F32)],
        scratch_shapes=[pltpu.VMEM((rows, D), F32), pltpu.VMEM((2, rows, SW2), F32),
                        pltpu.VMEM((rows, D), BF16), pltpu.VMEM((T, B, D), F32)],
        compiler_params=_params(vmem, 1),
        name="s5_mixer",
    )(x, g, sc, sh, bmat, cmat, a_r, a_i, dsk, s0r, s0i, g1, gate, wout, wout)


def _s5_weights(a_re, a_im, b_re, b_im, c_re, c_im, d_skip, log_dt):
    G, N, P = b_re.shape
    KW = MXU_DIM_V7X
    GP = KW // P
    KT, SW = G // GP, GP * N
    full = lambda *shape: pl.BlockSpec(shape, lambda: (0,) * len(shape))
    tile_n = jnp.tile(jnp.eye(N, dtype=BF16), (1, GP))
    tile_p = jnp.tile(jnp.eye(P, dtype=BF16), (1, GP))
    ab_re, ab_im, bmat, cmat = pl.pallas_call(
        _s5_disc_kernel,
        in_specs=[full(G, N), full(G, N), full(G, 1), full(G, P, N), full(G, P, N),
                  full(G, N, P), full(G, N, P), full(N, SW), full(P, KW)],
        out_specs=[full(G, N), full(G, N), full(KT, KW, 2 * SW), full(KT, 2 * SW, KW)],
        out_shape=[jax.ShapeDtypeStruct((G, N), F32), jax.ShapeDtypeStruct((G, N), F32),
                   jax.ShapeDtypeStruct((KT, KW, 2 * SW), BF16),
                   jax.ShapeDtypeStruct((KT, 2 * SW, KW), BF16)],
        compiler_params=pltpu.CompilerParams(vmem_limit_bytes=VMEM_CAP),
        name="s5_discretize",
    )(a_re, a_im, log_dt.reshape(G, 1), jnp.swapaxes(b_re, 1, 2), jnp.swapaxes(b_im, 1, 2),
      jnp.swapaxes(c_re, 1, 2), jnp.swapaxes(c_im, 1, 2), tile_n, tile_p)
    return (bmat, cmat, ab_re.reshape(KT, 1, SW), ab_im.reshape(KT, 1, SW),
            d_skip.reshape(1, G * P))


def _trunk(x, mod, row0, st_C, st_n, st_m, st_conv, st_re, st_im, p):
    B, L, D = x.shape
    depth = p['g_norm'].shape[0]
    new_C, new_n, new_m, new_conv, new_re, new_im = [], [], [], [], [], []
    for i in range(depth):
        m6 = mod[i, row0:row0 + B].reshape(B, 6, 1, D)
        sh1, sc1, g1, sh2, sc2, g2 = (m6[:, j] for j in range(6))
        gn = p['g_norm'][i]
        gvec = lambda r: gn[r].reshape(1, D)
        kind, j = i % N_MIXERS, i // N_MIXERS
        if kind == 0:
            H = st_C.shape[2]
            nz = p['wA_in'].shape[-1] - 2 * H
            w_gates = jnp.pad(p['wA_in'][j, :, nz:], ((0, 0), (0, LANES - 2 * H)))
            b_gates = jnp.pad(p['bA_gates'][j], (0, LANES - 2 * H)).reshape(1, LANES)
            z, gates = _inproj(x, gvec(0), sc1, sh1, p['wA_in'], j, nz, w_gates, b_gates)
            x, C, n, m = _mlstm_mixer(z, gates, p['gA_hnorm'][j].reshape(1, -1), st_C, j,
                                      st_n[j], st_m[j], x, gvec(1), g1, p['wA_out'])
            new_C.append(C); new_n.append(n); new_m.append(m)
        elif kind == 1:
            z3 = _inproj(x, gvec(0), sc1, sh1, p['wB_in'], j, p['wB_in'].shape[-1])
            x, cv = _conv_mixer(z3, p['wB_conv'][j], st_conv[j], x, gvec(1), g1,
                                p['wB_out'], j)
            new_conv.append(cv)
        else:
            bmat, cmat, a_r, a_i, dsk = _s5_weights(
                p['s5_A_re'][j], p['s5_A_im'][j], p['s5_B_re'][j], p['s5_B_im'][j],
                p['s5_C_re'][j], p['s5_C_im'][j], p['s5_D'][j], p['s5_log_dt'][j])
            KT, _, SW2 = bmat.shape
            to_lanes = lambda s: jnp.swapaxes(s.reshape(B, KT, SW2 // 2), 0, 1)
            x, sr, si = _s5_mixer(x, gvec(0), sc1.reshape(B, D), sh1.reshape(B, D), bmat, cmat,
                                  a_r, a_i, dsk, to_lanes(st_re[j]), to_lanes(st_im[j]),
                                  gvec(1), g1.reshape(B, D), p['wC_out'], j)
            from_lanes = lambda s: jnp.swapaxes(s, 0, 1).reshape(st_re[j].shape)
            new_re.append(from_lanes(sr)); new_im.append(from_lanes(si))
        x = _ffn(x, gvec(2), sc2, sh2, gvec(3), g2, p['w_ffn_gate'], p['w_ffn_up'],
                 p['w_ffn_down'], i)
    return (x, jnp.stack(new_C), jnp.stack(new_n), jnp.stack(new_m), jnp.stack(new_conv),
            jnp.stack(new_re), jnp.stack(new_im))


def kernel(x_prompt, x_sample, state_mlstm_C, state_mlstm_n, state_mlstm_m, state_conv,
           state_s5_re, state_s5_im, c_prompt, c_sample, w_mod, b_mod, g_norm, wA_in,
           bA_gates, gA_hnorm, wA_out, wB_in, wB_conv, wB_out, s5_A_re, s5_A_im, s5_B_re,
           s5_B_im, s5_C_re, s5_C_im, s5_D, s5_log_dt, wC_out, w_ffn_gate, w_ffn_up,
           w_ffn_down):
    cast = lambda w: w.astype(BF16)
    p = dict(g_norm=g_norm, wA_in=cast(wA_in), bA_gates=bA_gates, gA_hnorm=gA_hnorm,
             wA_out=cast(wA_out), wB_in=cast(wB_in), wB_conv=wB_conv, wB_out=cast(wB_out),
             s5_A_re=s5_A_re, s5_A_im=s5_A_im, s5_B_re=s5_B_re, s5_B_im=s5_B_im,
             s5_C_re=s5_C_re, s5_C_im=s5_C_im, s5_D=s5_D, s5_log_dt=s5_log_dt,
             wC_out=cast(wC_out), w_ffn_gate=cast(w_ffn_gate), w_ffn_up=cast(w_ffn_up),
             w_ffn_down=cast(w_ffn_down))
    bp = x_prompt.shape[0]
    mod = _modulation(jnp.concatenate([c_prompt, c_sample], axis=0), w_mod, b_mod)
    zeros = lambda s: jnp.zeros((s.shape[0], bp) + s.shape[2:], s.dtype)
    outs_p = _trunk(x_prompt, mod, 0, zeros(state_mlstm_C), zeros(state_mlstm_n),
                    zeros(state_mlstm_m), zeros(state_conv), zeros(state_s5_re),
                    zeros(state_s5_im), p)
    outs_s = _trunk(x_sample, mod, bp, state_mlstm_C, state_mlstm_n, state_mlstm_m,
                    state_conv, state_s5_re, state_s5_im, p)
    return (outs_p[0], outs_s[0]) + tuple(outs_p[1:]) + tuple(outs_s[1:])
```

```python
import functools

import jax
import jax.numpy as jnp
from jax import lax
from jax.experimental import pallas as pl
from jax.experimental.pallas import tpu as pltpu

F32 = jnp.float32
BF16 = jnp.bfloat16
EPS = 1e-6
N_MIXERS = 3

LANES = 128
SUBLANES = 8
MXU_DIM_V7X = 256
VMEM_BYTES_V7X = 64 * 1024 * 1024
VMEM_CAP = VMEM_BYTES_V7X - 6 * 1024 * 1024

ROW_TILE = 512
ROW_GROUP = MXU_DIM_V7X
FFN_ROW_TILE = 1024
INPROJ_ROW_TILE = 1024
INPROJ_COL_TILE = 2048
FFN_COL_TILE = 512
MOD_COL_TILE = 1024
MLSTM_CHUNK = 256
S5_CHUNK = 32
S5_LANE_GROUP = 4


def _params(vmem_bytes, n_grid):
    limit = int(min(VMEM_CAP, max(vmem_bytes * 5 // 4 + (4 << 20), 16 << 20)))
    return pltpu.CompilerParams(dimension_semantics=("arbitrary",) * n_grid,
                                vmem_limit_bytes=limit)


def _resident(block_shape, index_map):
    return pl.BlockSpec(block_shape, index_map, pipeline_mode=pl.Buffered(1))


def _row_blocking(B, L, tile=ROW_TILE):
    if L >= ROW_TILE:
        tile = min(tile, L)
        assert L % tile == 0
        return 1, tile
    assert L % SUBLANES == 0
    return B, L


def _norm_mod(x, g, sc, sh):
    ms = jnp.mean(x * x, axis=-1, keepdims=True)
    y = x * lax.rsqrt(ms + EPS) * g
    return y * (1.0 + sc) + sh


def _rms_gain(y, g):
    ms = jnp.mean(y * y, axis=-1, keepdims=True)
    return y * lax.rsqrt(ms + EPS) * g


def _row_groups(bt, tl):
    if bt != 1 or tl <= ROW_GROUP:
        return [(slice(0, tl), slice(0, bt * tl))]
    assert tl % ROW_GROUP == 0
    return [(slice(q * ROW_GROUP, (q + 1) * ROW_GROUP),) * 2 for q in range(tl // ROW_GROUP)]


def _mod_kernel(c_ref, w_ref, b_ref, o_ref):
    c = c_ref[...]
    sc = (c * jax.nn.sigmoid(c)).astype(BF16)
    o_ref[...] = jnp.dot(sc, w_ref[...].astype(BF16), preferred_element_type=F32) + b_ref[...]


def _modulation(c_all, w_mod, b_mod):
    depth, D, N = w_mod.shape
    R = c_all.shape[0]
    tn = min(MOD_COL_TILE, N)
    assert N % tn == 0
    vmem = 2 * D * tn * 4 + 2 * R * tn * 4 + R * D * 4
    return pl.pallas_call(
        _mod_kernel,
        grid=(depth, N // tn),
        in_specs=[_resident((R, D), lambda i, j: (0, 0)),
                  pl.BlockSpec((None, D, tn), lambda i, j: (i, 0, j)),
                  pl.BlockSpec((None, 1, tn), lambda i, j: (i, 0, j))],
        out_specs=pl.BlockSpec((None, R, tn), lambda i, j: (i, 0, j)),
        out_shape=jax.ShapeDtypeStruct((depth, R, N), F32),
        compiler_params=_params(vmem, 2),
        name="adaln_modulation",
    )(c_all, w_mod, b_mod.reshape(depth, 1, N))


def _inproj_kernel(x_ref, g_ref, sc_ref, sh_ref, w_ref, *rest, with_gates):
    if with_gates:
        wg_ref, bg_ref, z_ref, gates_ref, h_scr = rest
    else:
        z_ref, h_scr = rest
    bt, tl, D = x_ref.shape
    j = pl.program_id(2)

    @pl.when(j == 0)
    def _():
        for tsl, fr in _row_groups(bt, tl):
            h = _norm_mod(x_ref[:, tsl, :], g_ref[...], sc_ref[...], sh_ref[...])
            h2 = h.reshape(-1, D).astype(BF16)
            h_scr[fr, :] = h2
            z = jnp.dot(h2, w_ref[...], preferred_element_type=F32)
            z_ref[:, tsl, :] = z.reshape(bt, -1, z.shape[-1]).astype(z_ref.dtype)
            if with_gates:
                gates = jnp.dot(h2, wg_ref[...], preferred_element_type=F32) + bg_ref[...]
                gates_ref[:, tsl, :] = gates.reshape(bt, -1, gates.shape[-1])

    @pl.when(j > 0)
    def _():
        z = jnp.dot(h_scr[...], w_ref[...], preferred_element_type=F32)
        z_ref[...] = z.reshape(z_ref.shape).astype(z_ref.dtype)


def _inproj(x, g, sc, sh, w, layer, N, w_gates=None, b_gates=None):
    B, L, D = x.shape
    bt, tl = _row_blocking(B, L, INPROJ_ROW_TILE)
    rows = bt * tl
    tn = INPROJ_COL_TILE
    while N % tn:
        tn //= 2
    assert tn % LANES == 0
    with_gates = w_gates is not None
    row_map = lambda b, l, j: (b, l, 0)
    mod_map = lambda b, l, j: (b, 0, 0)
    in_specs = [pl.BlockSpec((bt, tl, D), row_map),
                _resident((1, D), lambda b, l, j: (0, 0)),
                pl.BlockSpec((bt, 1, D), mod_map),
                pl.BlockSpec((bt, 1, D), mod_map),
                pl.BlockSpec((None, D, tn), lambda b, l, j: (layer, 0, j))]
    args = [x, g, sc, sh, w]
    out_specs = [pl.BlockSpec((bt, tl, tn), lambda b, l, j: (b, l, j))]
    out_shape = [jax.ShapeDtypeStruct((B, L, N), BF16)]
    vmem = 2 * rows * D * 4 + 2 * D * tn * 2 + 2 * rows * tn * 2 + rows * D * 2
    if with_gates:
        in_specs += [_resident((D, LANES), lambda b, l, j: (0, 0)),
                     _resident((1, LANES), lambda b, l, j: (0, 0))]
        args += [w_gates, b_gates]
        out_specs.append(pl.BlockSpec((bt, tl, LANES), row_map))
        out_shape.append(jax.ShapeDtypeStruct((B, L, LANES), F32))
        vmem += D * LANES * 2 + 2 * rows * LANES * 4
    out = pl.pallas_call(
        functools.partial(_inproj_kernel, with_gates=with_gates),
        grid=(B // bt, L // tl, N // tn),
        in_specs=in_specs,
        out_specs=out_specs,
        out_shape=out_shape,
        scratch_shapes=[pltpu.VMEM((rows, D), BF16)],
        compiler_params=_params(vmem, 3),
        name="norm_mod_inproj",
    )(*args)
    return out if with_gates else out[0]


def _ffn_kernel(x_ref, g2_ref, sc_ref, sh_ref, g3_ref, gate_ref, wg_ref, wu_ref, wd_ref,
                o_ref, h_scr):
    bt, tl, D = x_ref.shape
    f = pl.program_id(2)
    last = pl.num_programs(2) - 1

    def partial_ffn(h2):
        gg = jnp.dot(h2, wg_ref[...], preferred_element_type=F32)
        uu = jnp.dot(h2, wu_ref[...], preferred_element_type=F32)
        act = (gg * jax.nn.sigmoid(gg) * uu).astype(BF16)
        return jnp.dot(act, wd_ref[...], preferred_element_type=F32).reshape(bt, -1, D)

    @pl.when(f == 0)
    def _():
        for tsl, fr in _row_groups(bt, tl):
            h = _norm_mod(x_ref[:, tsl, :], g2_ref[...], sc_ref[...], sh_ref[...])
            h2 = h.reshape(-1, D).astype(BF16)
            h_scr[fr, :] = h2
            o_ref[:, tsl, :] = partial_ffn(h2)

    @pl.when(jnp.logical_and(f > 0, f < last))
    def _():
        o_ref[...] += partial_ffn(h_scr[...])

    @pl.when(f == last)
    def _():
        for tsl, fr in _row_groups(bt, tl):
            y = o_ref[:, tsl, :] + partial_ffn(h_scr[fr, :])
            o_ref[:, tsl, :] = x_ref[:, tsl, :] + gate_ref[...] * _rms_gain(y, g3_ref[...])


def _ffn(x, g2, sc, sh, g3, gate, wg, wu, wd, layer):
    B, L, D = x.shape
    F = wg.shape[-1]
    bt, tl = _row_blocking(B, L, FFN_ROW_TILE)
    rows = bt * tl
    tf = FFN_COL_TILE
    assert F % tf == 0 and F // tf >= 2
    row_map = lambda b, l, f: (b, l, 0)
    mod_map = lambda b, l, f: (b, 0, 0)
    vec = lambda: _resident((1, D), lambda b, l, f: (0, 0))
    vmem = 4 * rows * D * 4 + 3 * 2 * D * tf * 2 + rows * D * 2 + 3 * rows * tf * 4
    return pl.pallas_call(
        _ffn_kernel,
        grid=(B // bt, L // tl, F // tf),
        in_specs=[pl.BlockSpec((bt, tl, D), row_map), vec(),
                  pl.BlockSpec((bt, 1, D), mod_map), pl.BlockSpec((bt, 1, D), mod_map),
                  vec(), pl.BlockSpec((bt, 1, D), mod_map),
                  pl.BlockSpec((None, D, tf), lambda b, l, f: (layer, 0, f)),
                  pl.BlockSpec((None, D, tf), lambda b, l, f: (layer, 0, f)),
                  pl.BlockSpec((None, tf, D), lambda b, l, f: (layer, f, 0))],
        out_specs=pl.BlockSpec((bt, tl, D), row_map),
        out_shape=jax.ShapeDtypeStruct((B, L, D), F32),
        scratch_shapes=[pltpu.VMEM((rows, D), BF16)],
        compiler_params=_params(vmem, 3),
        name="swiglu_ffn",
    )(x, g2, sc, sh, g3, gate, wg, wu, wd)


def _log_sigmoid(x):
    return -(jnp.maximum(-x, 0.0) + jnp.log1p(jnp.exp(-jnp.abs(x))))


def _cumsum_rows(x):
    n = x.shape[0]
    row = lax.broadcasted_iota(jnp.int32, x.shape, 0)
    s = 1
    while s < n:
        x = x + jnp.where(row >= s, pltpu.roll(x, s, 0), 0.0)
        s *= 2
    return x


def _mlstm_kernel(q_ref, k_ref, v_ref, o_ref, gt_ref, ghn_ref, *rest, valid_len, has_state):
    if has_state:
        c0_ref, n0_ref, m0_ref, *rest = rest
    x_ref, g1_ref, gate_ref, wout_ref, xo_ref, c_ref, n_ref, m_ref, a_scr = rest
    H, DK, DV = c_ref.shape
    Lc = q_ref.shape[0]
    scale = DK ** -0.5
    c = pl.program_id(1)
    scanning = c < pl.num_programs(1) - 1

    @pl.when(c == 0)
    def _():
        if has_state:
            c_ref[...] = c0_ref[...]
            n_ref[...] = n0_ref[...]
            m_ref[...] = m0_ref[...]
        else:
            c_ref[...] = jnp.zeros_like(c_ref)
            n_ref[...] = jnp.zeros_like(n_ref)
            m_ref[...] = jnp.zeros_like(m_ref)
        a_scr[...] = jnp.zeros_like(a_scr)

    gl = gt_ref[...]
    li_all = gl
    lf_all = _log_sigmoid(gl)
    if valid_len < Lc:
        valid = lax.broadcasted_iota(jnp.int32, gl.shape, 0) < valid_len
        li_all = jnp.where(valid, li_all, -jnp.inf)
        lf_all = jnp.where(valid, lf_all, 0.0)
    b_all = _cumsum_rows(lf_all)
    causal = (lax.broadcasted_iota(jnp.int32, (Lc, Lc), 0)
              >= lax.broadcasted_iota(jnp.int32, (Lc, Lc), 1))

    heads = range(H)
    q = [q_ref[:, h * DK:(h + 1) * DK] for h in heads]
    k = [k_ref[:, h * DK:(h + 1) * DK] for h in heads]
    v = [v_ref[:, h * DV:(h + 1) * DV] for h in heads]
    C = [c_ref[h] for h in heads]
    n = [n_ref[h] for h in heads]
    m = [m_ref[h] for h in heads]

    qk, qC, y = [], [], None
    for h in heads:
        qk.append(lax.dot_general(q[h], k[h], (((1,), (1,)), ((), ())),
                                  preferred_element_type=F32))
        qC.append(jnp.dot(q[h], C[h].astype(BF16), preferred_element_type=F32))
        yh = jnp.dot(a_scr[:, h * DV:(h + 1) * DV], wout_ref[h * DV:(h + 1) * DV, :],
                     preferred_element_type=F32)
        y = yh if y is None else y + yh
    xo_ref[...] = x_ref[...] + gate_ref[...] * _rms_gain(y, g1_ref[...])

    b, g, inter, m_t, w = [], [], [], [], []
    for h in heads:
        b.append(b_all[:, H + h:H + h + 1])
        g.append(li_all[:, h:h + 1] - b[h])
        g_row = jnp.transpose(jnp.broadcast_to(g[h], (Lc, LANES)))[0:1, :]
        dmat = jnp.where(causal, b[h] + g_row, -jnp.inf)
        inter.append(b[h] + m[h])
        m_t.append(jnp.maximum(inter[h], jnp.max(dmat, axis=-1, keepdims=True)))
        w.append(jnp.exp(dmat - m_t[h]))

    for h in heads:
        s = qk[h] * scale * w[h]
        den_s = jnp.sum(s, axis=-1, keepdims=True)
        sv = jnp.dot(s.astype(BF16), v[h], preferred_element_type=F32)
        m_new = m_t[h][Lc - 1:Lc, :]
        w_last = jnp.exp(b[h][Lc - 1:Lc, :] + g[h] - m_new)
        decay = jnp.exp(inter[h][Lc - 1:Lc, :] - m_new)
        kw = k[h].astype(F32) * w_last
        c_new = decay * C[h] + lax.dot_general(
            kw.astype(BF16), v[h], (((0,), (0,)), ((), ())), preferred_element_type=F32)
        c_ref[h] = jnp.where(scanning, c_new, C[h])
        n_ref[h] = jnp.where(scanning, decay * n[h] + jnp.sum(kw, axis=0, keepdims=True), n[h])
        m_ref[h] = jnp.where(scanning, m_new, m[h])

        inter_w = jnp.exp(inter[h] - m_t[h]) * scale
        num = sv + inter_w * qC[h]
        den = den_s + inter_w * jnp.sum(q[h].astype(F32) * n[h], axis=-1, keepdims=True)
        floor = jnp.maximum(jnp.abs(den), jnp.exp(-m_t[h]))
        hh = num * (1.0 / floor)
        hn = _rms_gain(hh, ghn_ref[:, h * DV:(h + 1) * DV])
        og = o_ref[:, h * DV:(h + 1) * DV].astype(F32)
        a_scr[:, h * DV:(h + 1) * DV] = (hn * jax.nn.sigmoid(og)).astype(BF16)


def _mlstm_mixer(z, gates, ghn, dims, state, layer, x, g1, gate, wout):
    B, L, D = x.shape
    H, DK, DV = dims
    HK, HV = H * DK, H * DV
    assert HV == 2 * HK and z.shape[-1] == 2 * HK + 2 * HV
    valid_len = L
    if L >= MLSTM_CHUNK:
        Lc = MLSTM_CHUNK
        assert L % Lc == 0
    else:
        Lc = LANES
        pad = ((0, 0), (0, Lc - L), (0, 0))
        z, gates, x = jnp.pad(z, pad), jnp.pad(gates, pad), jnp.pad(x, pad)
    nc = z.shape[1] // Lc
    blk = lambda width, idx: pl.BlockSpec(
        (None, Lc, width), lambda b, c: (b, jnp.minimum(c, nc - 1), idx))
    fin = pl.BlockSpec((None, Lc, D), lambda b, c: (b, jnp.maximum(c - 1, 0), 0))
    st3 = lambda d1, d2: pl.BlockSpec((None, H, d1, d2), lambda b, c: (b, 0, 0, 0))
    vmem = (2 * Lc * (2 * HK + 2 * HV) * 2 + 2 * Lc * LANES * 4 + 4 * Lc * D * 4
            + 4 * H * DK * DV * 4 + 12 * Lc * Lc * 4 + 8 * Lc * DV * 4 + 2 * DK * DV * 4
            + HV * D * 2 + 3 * Lc * D * 4)
    state_specs, state_args = [], []
    if state is not None:
        C0, n0, m0 = state
        state_specs = [pl.BlockSpec((None, None, H, DK, DV), lambda b, c: (layer, b, 0, 0, 0)),
                       st3(1, DK), st3(1, 1)]
        state_args = [C0, n0.reshape(B, H, 1, DK), m0.reshape(B, H, 1, 1)]
    xo, C, n, m = pl.pallas_call(
        functools.partial(_mlstm_kernel, valid_len=valid_len, has_state=state is not None),
        grid=(B, nc + 1),
        in_specs=[blk(HK, 0), blk(HK, 1), blk(HV, 1), blk(HV, 2), blk(LANES, 0),
                  _resident((1, HV), lambda b, c: (0, 0)), *state_specs,
                  fin, _resident((1, D), lambda b, c: (0, 0)),
                  pl.BlockSpec((None, 1, D), lambda b, c: (b, 0, 0)),
                  _resident((None, HV, D), lambda b, c: (layer, 0, 0))],
        out_specs=[fin, st3(DK, DV), st3(1, DK), st3(1, 1)],
        out_shape=[jax.ShapeDtypeStruct((B, nc * Lc, D), F32),
                   jax.ShapeDtypeStruct((B, H, DK, DV), F32),
                   jax.ShapeDtypeStruct((B, H, 1, DK), F32),
                   jax.ShapeDtypeStruct((B, H, 1, 1), F32)],
        scratch_shapes=[pltpu.VMEM((Lc, HV), BF16)],
        compiler_params=_params(vmem, 2),
        name="mlstm_mixer",
    )(z, z, z, z, gates, ghn, *state_args, x, g1, gate, wout)
    return xo[:, :L], C, n.reshape(B, H, DK), m.reshape(B, H)


def _conv_kernel(gb_ref, gc_ref, u_ref, w_ref, prev_ref, x_ref, g1_ref, gate_ref, wout_ref,
                 xo_ref, st_ref):
    tl, D = gc_ref.shape
    W = w_ref.shape[0]

    @pl.when(pl.program_id(1) == 0)
    def _():
        st_ref[...] = prev_ref[...]

    z = gc_ref[...].astype(F32) * u_ref[...].astype(F32)
    row = lax.broadcasted_iota(jnp.int32, (tl, D), 0)
    conv = z * w_ref[W - 1:W, :]
    for d in range(1, W):
        zd = pltpu.roll(z, d, 0)
        for r in range(d):
            zd = jnp.where(row == r, st_ref[W - 1 - d + r:W - d + r, :], zd)
        conv = conv + zd * w_ref[W - 1 - d:W - d, :]
    a = (gb_ref[...].astype(F32) * conv).astype(BF16)
    st_ref[...] = z[tl - (W - 1):, :]
    y = jnp.dot(a, wout_ref[...], preferred_element_type=F32)
    xo_ref[...] = x_ref[...] + gate_ref[...] * _rms_gain(y, g1_ref[...])


def _conv_mixer(z3, w_conv, prev, x, g1, gate, wout, layer):
    B, L, D = x.shape
    W = w_conv.shape[1]
    tl = min(L, ROW_TILE)
    assert L % tl == 0 and tl >= W - 1 and z3.shape[-1] == 3 * D
    blk = lambda idx: pl.BlockSpec((None, tl, D), lambda b, l: (b, l, idx))
    st = pl.BlockSpec((None, W - 1, D), lambda b, l: (b, 0, 0))
    vmem = 2 * 3 * tl * D * 2 + 4 * tl * D * 4 + D * D * 2 + 6 * tl * D * 4
    return pl.pallas_call(
        _conv_kernel,
        grid=(B, L // tl),
        in_specs=[blk(0), blk(1), blk(2), _resident((W, D), lambda b, l: (0, 0)), st,
                  blk(0), _resident((1, D), lambda b, l: (0, 0)),
                  pl.BlockSpec((None, 1, D), lambda b, l: (b, 0, 0)),
                  _resident((None, D, D), lambda b, l: (layer, 0, 0))],
        out_specs=[blk(0), st],
        out_shape=[jax.ShapeDtypeStruct((B, L, D), F32),
                   jax.ShapeDtypeStruct((B, W - 1, D), prev.dtype)],
        compiler_params=_params(vmem, 2),
        name="conv_mixer",
    )(z3, z3, z3, jnp.transpose(w_conv), prev, x, g1, gate, wout)


def _s5_disc_kernel(ar_ref, ai_ref, ldt_ref, br_ref, bi_ref, cr_ref, ci_ref, tn_ref, tp_ref,
                    abr_ref, abi_ref, bm_ref, cm_ref):
    G, P, N = br_ref.shape
    KT, KW, SW2 = bm_ref.shape
    GP, SW = KW // P, SW2 // 2
    dt = jnp.exp(ldt_ref[...])
    lr, lim = ar_ref[...], ai_ref[...]
    mag = jnp.exp(lr * dt)
    ab_re, ab_im = mag * jnp.cos(lim * dt), mag * jnp.sin(lim * dt)
    den = lr * lr + lim * lim
    nr = ab_re - 1.0
    fr = (nr * lr + ab_im * lim) / den
    fi = (ab_im * lr - nr * lim) / den
    abr_ref[...] = ab_re
    abi_ref[...] = ab_im
    br, bi = br_ref[...], bi_ref[...]
    bb = (fr[:, None, :] * br - fi[:, None, :] * bi, fr[:, None, :] * bi + fi[:, None, :] * br)
    cc = (cr_ref[...], -ci_ref[...])

    b_same = (lax.broadcasted_iota(jnp.int32, (KW, SW), 0) // P
              == lax.broadcasted_iota(jnp.int32, (KW, SW), 1) // N)
    c_same = (lax.broadcasted_iota(jnp.int32, (SW, KW), 0) // N
              == lax.broadcasted_iota(jnp.int32, (SW, KW), 1) // P)
    for kt in range(KT):
        for ri in range(2):
            bk = bb[ri][kt * GP:(kt + 1) * GP].reshape(KW, N).astype(BF16)
            spread = jnp.dot(bk, tn_ref[...], preferred_element_type=F32)
            bm_ref[kt, :, ri * SW:(ri + 1) * SW] = jnp.where(b_same, spread, 0.0).astype(BF16)
            ck = cc[ri][kt * GP:(kt + 1) * GP].reshape(SW, P).astype(BF16)
            spread = jnp.dot(ck, tp_ref[...], preferred_element_type=F32)
            cm_ref[kt, ri * SW:(ri + 1) * SW, :] = jnp.where(c_same, spread, 0.0).astype(BF16)


def _s5_kernel(x_ref, g_ref, sc_ref, sh_ref, bm_ref, cm_ref, ar_ref, ai_ref, dsk_ref,
               s0r_ref, s0i_ref, g1_ref, gate_ref, wa_ref, wb_ref,
               xo_ref, sr_ref, si_ref, h_scr, bu_scr, y_scr, xt_scr):
    B, T, D = x_ref.shape
    KT, KW, SW2 = bm_ref.shape
    SW = SW2 // 2
    rows = T * B
    LG = S5_LANE_GROUP * LANES
    c = pl.program_id(0)
    scanning = c < pl.num_programs(0) - 1

    @pl.when(c == 0)
    def _():
        sr_ref[...] = s0r_ref[...]
        si_ref[...] = s0i_ref[...]
        y_scr[...] = jnp.zeros_like(y_scr)
        xt_scr[...] = jnp.zeros_like(xt_scr)

    yg_prev = y_scr[...]
    xt_prev = xt_scr[...]

    xt = pltpu.einshape("btd->tbd", x_ref[...])
    xt_scr[...] = xt
    h_scr[...] = _norm_mod(xt, g_ref[...], sc_ref[...], sh_ref[...]).reshape(rows, D)
    z_cols = []

    for kt in range(KT):
        cols = slice(kt * KW, (kt + 1) * KW)
        bu = bu_scr.at[kt % 2]
        hk = h_scr[:, cols]
        bu[...] = jnp.dot(hk.astype(BF16), bm_ref[kt], preferred_element_type=F32)
        z_cols.append(jnp.dot(yg_prev, wa_ref[:, cols], preferred_element_type=F32)
                      * jax.nn.sigmoid(jnp.dot(yg_prev, wb_ref[:, cols],
                                               preferred_element_type=F32)))
        for lg in range(SW // LG):
            re_cols = slice(lg * LG, (lg + 1) * LG)
            im_cols = slice(SW + lg * LG, SW + (lg + 1) * LG)
            a_r = jnp.broadcast_to(ar_ref[kt, :, re_cols], (B, LG))
            a_i = jnp.broadcast_to(ai_ref[kt, :, re_cols], (B, LG))
            xr0, xi0 = sr_ref[kt, :, re_cols], si_ref[kt, :, re_cols]
            xr, xi = xr0, xi0
            for t in range(T):
                r = slice(t * B, (t + 1) * B)
                xr, xi = (a_r * xr - a_i * xi + bu[r, re_cols],
                          a_r * xi + a_i * xr + bu[r, im_cols])
                bu[r, re_cols] = xr
                bu[r, im_cols] = xi
            sr_ref[kt, :, re_cols] = jnp.where(scanning, xr, xr0)
            si_ref[kt, :, re_cols] = jnp.where(scanning, xi, xi0)
        yk = jnp.dot(bu[...].astype(BF16), cm_ref[kt], preferred_element_type=F32)
        yk = yk + dsk_ref[:, cols] * hk
        y_scr[:, cols] = jax.nn.gelu(yk).astype(y_scr.dtype)

    z = jnp.concatenate(z_cols, axis=1)
    out = xt_prev + gate_ref[...] * _rms_gain(z, g1_ref[...]).reshape(T, B, D)
    xo_ref[...] = pltpu.einshape("tbd->btd", out)


def _s5_mixer(x, g, sc, sh, bmat, cmat, a_r, a_i, dsk, s0r, s0i, g1, gate, wout, layer):
    B, L, D = x.shape
    KT, KW, SW2 = bmat.shape
    SW = SW2 // 2
    assert B == SUBLANES and SW % (S5_LANE_GROUP * LANES) == 0
    T = min(S5_CHUNK, L)
    assert L % T == 0
    rows = T * B
    c0 = lambda *shape: _resident(shape, lambda c: (0,) * len(shape))
    vmem = (4 * rows * D * 4 + 2 * KT * KW * SW2 * 2 + 2 * D * D * 2 + rows * D * 4
            + 2 * rows * SW2 * 4 + rows * D * 2 + 5 * rows * D * 4 + 6 * KT * B * SW * 4)
    nc = L // T
    return pl.pallas_call(
        _s5_kernel,
        grid=(nc + 1,),
        in_specs=[pl.BlockSpec((B, T, D), lambda c: (0, jnp.minimum(c, nc - 1), 0)),
                  c0(1, D), c0(B, D), c0(B, D),
                  c0(KT, KW, SW2), c0(KT, SW2, KW), c0(KT, 1, SW), c0(KT, 1, SW), c0(1, D),
                  c0(KT, B, SW), c0(KT, B, SW), c0(1, D), c0(B, D),
                  _resident((None, D, D), lambda c: (layer, 0, 0)),
                  _resident((None, D, D), lambda c: (layer, 0, 1))],
        out_specs=[pl.BlockSpec((B, T, D), lambda c: (0, jnp.maximum(c - 1, 0), 0)),
                   pl.BlockSpec((KT, B, SW), lambda c: (0, 0, 0)),
                   pl.BlockSpec((KT, B, SW), lambda c: (0, 0, 0))],
        out_shape=[jax.ShapeDtypeStruct((B, L, D), F32),
                   jax.ShapeDtypeStruct((KT, B, SW), F32),
                   jax.ShapeDtypeStruct((KT, B, SW), F32)],
        scratch_shapes=[pltpu.VMEM((rows, D), F32), pltpu.VMEM((2, rows, SW2), F32),
                        pltpu.VMEM((rows, D), BF16), pltpu.VMEM((T, B, D), F32)],
        compiler_params=_params(vmem, 1),
        name="s5_mixer",
    )(x, g, sc, sh, bmat, cmat, a_r, a_i, dsk, s0r, s0i, g1, gate, wout, wout)


def _s5_weights(a_re, a_im, b_re, b_im, c_re, c_im, d_skip, log_dt):
    G, N, P = b_re.shape
    KW = MXU_DIM_V7X
    GP = KW // P
    KT, SW = G // GP, GP * N
    full = lambda *shape: pl.BlockSpec(shape, lambda: (0,) * len(shape))
    tile_n = jnp.tile(jnp.eye(N, dtype=BF16), (1, GP))
    tile_p = jnp.tile(jnp.eye(P, dtype=BF16), (1, GP))
    ab_re, ab_im, bmat, cmat = pl.pallas_call(
        _s5_disc_kernel,
        in_specs=[full(G, N), full(G, N), full(G, 1), full(G, P, N), full(G, P, N),
                  full(G, N, P), full(G, N, P), full(N, SW), full(P, KW)],
        out_specs=[full(G, N), full(G, N), full(KT, KW, 2 * SW), full(KT, 2 * SW, KW)],
        out_shape=[jax.ShapeDtypeStruct((G, N), F32), jax.ShapeDtypeStruct((G, N), F32),
                   jax.ShapeDtypeStruct((KT, KW, 2 * SW), BF16),
                   jax.ShapeDtypeStruct((KT, 2 * SW, KW), BF16)],
        compiler_params=pltpu.CompilerParams(vmem_limit_bytes=VMEM_CAP),
        name="s5_discretize",
    )(a_re, a_im, log_dt.reshape(G, 1), jnp.swapaxes(b_re, 1, 2), jnp.swapaxes(b_im, 1, 2),
      jnp.swapaxes(c_re, 1, 2), jnp.swapaxes(c_im, 1, 2), tile_n, tile_p)
    return (bmat, cmat, ab_re.reshape(KT, 1, SW), ab_im.reshape(KT, 1, SW),
            d_skip.reshape(1, G * P))


def _trunk(x, mod, row0, st_C, st_n, st_m, st_conv, st_re, st_im, p):
    B, L, D = x.shape
    depth = p['g_norm'].shape[0]
    new_C, new_n, new_m, new_conv, new_re, new_im = [], [], [], [], [], []
    for i in range(depth):
        m6 = mod[i, row0:row0 + B].reshape(B, 6, 1, D)
        sh1, sc1, g1, sh2, sc2, g2 = (m6[:, j] for j in range(6))
        gn = p['g_norm'][i]
        gvec = lambda r: gn[r].reshape(1, D)
        kind, j = i % N_MIXERS, i // N_MIXERS
        if kind == 0:
            H = p['bA_gates'].shape[-1] // 2
            HV = p['wA_out'].shape[1]
            nz = p['wA_in'].shape[-1] - 2 * H
            dims = (H, (nz - 2 * HV) // (2 * H), HV // H)
            state = None if st_C is None else (st_C, st_n[j], st_m[j])
            w_gates = jnp.pad(p['wA_in'][j, :, nz:], ((0, 0), (0, LANES - 2 * H)))
            b_gates = jnp.pad(p['bA_gates'][j], (0, LANES - 2 * H)).reshape(1, LANES)
            z, gates = _inproj(x, gvec(0), sc1, sh1, p['wA_in'], j, nz, w_gates, b_gates)
            x, C, n, m = _mlstm_mixer(z, gates, p['gA_hnorm'][j].reshape(1, -1), dims, state, j,
                                      x, gvec(1), g1, p['wA_out'])
            new_C.append(C); new_n.append(n); new_m.append(m)
        elif kind == 1:
            z3 = _inproj(x, gvec(0), sc1, sh1, p['wB_in'], j, p['wB_in'].shape[-1])
            x, cv = _conv_mixer(z3, p['wB_conv'][j], st_conv[j], x, gvec(1), g1,
                                p['wB_out'], j)
            new_conv.append(cv)
        else:
            bmat, cmat, a_r, a_i, dsk = _s5_weights(
                p['s5_A_re'][j], p['s5_A_im'][j], p['s5_B_re'][j], p['s5_B_im'][j],
                p['s5_C_re'][j], p['s5_C_im'][j], p['s5_D'][j], p['s5_log_dt'][j])
            KT, _, SW2 = bmat.shape
            to_lanes = lambda s: jnp.swapaxes(s.reshape(B, KT, SW2 // 2), 0, 1)
            x, sr, si = _s5_mixer(x, gvec(0), sc1.reshape(B, D), sh1.reshape(B, D), bmat, cmat,
                                  a_r, a_i, dsk, to_lanes(st_re[j]), to_lanes(st_im[j]),
                                  gvec(1), g1.reshape(B, D), p['wC_out'], j)
            from_lanes = lambda s: jnp.swapaxes(s, 0, 1).reshape(st_re[j].shape)
            new_re.append(from_lanes(sr)); new_im.append(from_lanes(si))
        x = _ffn(x, gvec(2), sc2, sh2, gvec(3), g2, p['w_ffn_gate'], p['w_ffn_up'],
                 p['w_ffn_down'], i)
    return (x, jnp.stack(new_C), jnp.stack(new_n), jnp.stack(new_m), jnp.stack(new_conv),
            jnp.stack(new_re), jnp.stack(new_im))


def kernel(x_prompt, x_sample, state_mlstm_C, state_mlstm_n, state_mlstm_m, state_conv,
           state_s5_re, state_s5_im, c_prompt, c_sample, w_mod, b_mod, g_norm, wA_in,
           bA_gates, gA_hnorm, wA_out, wB_in, wB_conv, wB_out, s5_A_re, s5_A_im, s5_B_re,
           s5_B_im, s5_C_re, s5_C_im, s5_D, s5_log_dt, wC_out, w_ffn_gate, w_ffn_up,
           w_ffn_down):
    cast = lambda w: w.astype(BF16)
    p = dict(g_norm=g_norm, wA_in=cast(wA_in), bA_gates=bA_gates, gA_hnorm=gA_hnorm,
             wA_out=cast(wA_out), wB_in=cast(wB_in), wB_conv=wB_conv, wB_out=cast(wB_out),
             s5_A_re=s5_A_re, s5_A_im=s5_A_im, s5_B_re=s5_B_re, s5_B_im=s5_B_im,
             s5_C_re=s5_C_re, s5_C_im=s5_C_im, s5_D=s5_D, s5_log_dt=s5_log_dt,
             wC_out=cast(wC_out), w_ffn_gate=cast(w_ffn_gate), w_ffn_up=cast(w_ffn_up),
             w_ffn_down=cast(w_ffn_down))
    bp = x_prompt.shape[0]
    mod = _modulation(jnp.concatenate([c_prompt, c_sample], axis=0), w_mod, b_mod)
    zeros = lambda s: jnp.zeros((s.shape[0], bp) + s.shape[2:], s.dtype)
    outs_p = _trunk(x_prompt, mod, 0, None, None, None, zeros(state_conv),
                    zeros(state_s5_re), zeros(state_s5_im), p)
    outs_s = _trunk(x_sample, mod, bp, state_mlstm_C, state_mlstm_n, state_mlstm_m,
                    state_conv, state_s5_re, state_s5_im, p)
    return (outs_p[0], outs_s[0]) + tuple(outs_p[1:]) + tuple(outs_s[1:])
```

```python
import functools

import jax
import jax.numpy as jnp
from jax import lax
from jax.experimental import pallas as pl
from jax.experimental.pallas import tpu as pltpu

F32 = jnp.float32
BF16 = jnp.bfloat16
EPS = 1e-6
N_MIXERS = 3

LANES = 128
SUBLANES = 8
MXU_DIM_V7X = 256
VMEM_BYTES_V7X = 64 * 1024 * 1024
VMEM_CAP = VMEM_BYTES_V7X - 6 * 1024 * 1024

ROW_TILE = 512
ROW_GROUP = MXU_DIM_V7X
FFN_ROW_TILE = 1024
INPROJ_ROW_TILE = 1024
INPROJ_COL_TILE = 2048
FFN_COL_TILE = 512
MOD_COL_TILE = 1024
MLSTM_CHUNK = 256
S5_CHUNK = 32
S5_LANE_GROUP = 4


def _params(vmem_bytes, n_grid):
    limit = int(min(VMEM_CAP, max(vmem_bytes * 5 // 4 + (4 << 20), 16 << 20)))
    semantics = ("parallel",) * (n_grid - 1) + ("arbitrary",)
    return pltpu.CompilerParams(dimension_semantics=semantics, vmem_limit_bytes=limit)


def _resident(block_shape, index_map):
    return pl.BlockSpec(block_shape, index_map, pipeline_mode=pl.Buffered(1))


def _row_blocking(B, L, tile=ROW_TILE):
    if L >= ROW_TILE:
        tile = min(tile, L)
        assert L % tile == 0
        return 1, tile
    assert L % SUBLANES == 0
    return B, L


def _norm_mod(x, g, sc, sh):
    ms = jnp.mean(x * x, axis=-1, keepdims=True)
    y = x * lax.rsqrt(ms + EPS) * g
    return y * (1.0 + sc) + sh


def _rms_gain(y, g):
    ms = jnp.mean(y * y, axis=-1, keepdims=True)
    return y * lax.rsqrt(ms + EPS) * g


def _row_groups(bt, tl):
    if bt != 1 or tl <= ROW_GROUP:
        return [(slice(0, tl), slice(0, bt * tl))]
    assert tl % ROW_GROUP == 0
    return [(slice(q * ROW_GROUP, (q + 1) * ROW_GROUP),) * 2 for q in range(tl // ROW_GROUP)]


def _mod_kernel(c_ref, w_ref, b_ref, o_ref):
    c = c_ref[...]
    sc = (c * jax.nn.sigmoid(c)).astype(BF16)
    o_ref[...] = jnp.dot(sc, w_ref[...].astype(BF16), preferred_element_type=F32) + b_ref[...]


def _modulation(c_all, w_mod, b_mod):
    depth, D, N = w_mod.shape
    R = c_all.shape[0]
    tn = min(MOD_COL_TILE, N)
    assert N % tn == 0
    vmem = 2 * D * tn * 4 + 2 * R * tn * 4 + R * D * 4
    return pl.pallas_call(
        _mod_kernel,
        grid=(depth, N // tn),
        in_specs=[_resident((R, D), lambda i, j: (0, 0)),
                  pl.BlockSpec((None, D, tn), lambda i, j: (i, 0, j)),
                  pl.BlockSpec((None, 1, tn), lambda i, j: (i, 0, j))],
        out_specs=pl.BlockSpec((None, R, tn), lambda i, j: (i, 0, j)),
        out_shape=jax.ShapeDtypeStruct((depth, R, N), F32),
        compiler_params=_params(vmem, 2),
        name="adaln_modulation",
    )(c_all, w_mod, b_mod.reshape(depth, 1, N))


def _inproj_kernel(x_ref, g_ref, sc_ref, sh_ref, w_ref, *rest, with_gates):
    if with_gates:
        wg_ref, bg_ref, z_ref, gates_ref, h_scr = rest
    else:
        z_ref, h_scr = rest
    bt, tl, D = x_ref.shape
    j = pl.program_id(2)

    @pl.when(j == 0)
    def _():
        for tsl, fr in _row_groups(bt, tl):
            h = _norm_mod(x_ref[:, tsl, :], g_ref[...], sc_ref[...], sh_ref[...])
            h2 = h.reshape(-1, D).astype(BF16)
            h_scr[fr, :] = h2
            z = jnp.dot(h2, w_ref[...], preferred_element_type=F32)
            z_ref[:, tsl, :] = z.reshape(bt, -1, z.shape[-1]).astype(z_ref.dtype)
            if with_gates:
                gates = jnp.dot(h2, wg_ref[...], preferred_element_type=F32) + bg_ref[...]
                gates_ref[:, tsl, :] = gates.reshape(bt, -1, gates.shape[-1])

    @pl.when(j > 0)
    def _():
        z = jnp.dot(h_scr[...], w_ref[...], preferred_element_type=F32)
        z_ref[...] = z.reshape(z_ref.shape).astype(z_ref.dtype)


def _inproj(x, g, sc, sh, w, layer, N, w_gates=None, b_gates=None):
    B, L, D = x.shape
    bt, tl = _row_blocking(B, L, INPROJ_ROW_TILE)
    rows = bt * tl
    tn = INPROJ_COL_TILE
    while N % tn:
        tn //= 2
    assert tn % LANES == 0
    with_gates = w_gates is not None
    row_map = lambda b, l, j: (b, l, 0)
    mod_map = lambda b, l, j: (b, 0, 0)
    in_specs = [pl.BlockSpec((bt, tl, D), row_map),
                _resident((1, D), lambda b, l, j: (0, 0)),
                pl.BlockSpec((bt, 1, D), mod_map),
                pl.BlockSpec((bt, 1, D), mod_map),
                pl.BlockSpec((None, D, tn), lambda b, l, j: (layer, 0, j))]
    args = [x, g, sc, sh, w]
    out_specs = [pl.BlockSpec((bt, tl, tn), lambda b, l, j: (b, l, j))]
    out_shape = [jax.ShapeDtypeStruct((B, L, N), BF16)]
    vmem = 2 * rows * D * 4 + 2 * D * tn * 2 + 2 * rows * tn * 2 + rows * D * 2
    if with_gates:
        in_specs += [_resident((D, LANES), lambda b, l, j: (0, 0)),
                     _resident((1, LANES), lambda b, l, j: (0, 0))]
        args += [w_gates, b_gates]
        out_specs.append(pl.BlockSpec((bt, tl, LANES), row_map))
        out_shape.append(jax.ShapeDtypeStruct((B, L, LANES), F32))
        vmem += D * LANES * 2 + 2 * rows * LANES * 4
    out = pl.pallas_call(
        functools.partial(_inproj_kernel, with_gates=with_gates),
        grid=(B // bt, L // tl, N // tn),
        in_specs=in_specs,
        out_specs=out_specs,
        out_shape=out_shape,
        scratch_shapes=[pltpu.VMEM((rows, D), BF16)],
        compiler_params=_params(vmem, 3),
        name="norm_mod_inproj",
    )(*args)
    return out if with_gates else out[0]


def _ffn_kernel(x_ref, g2_ref, sc_ref, sh_ref, g3_ref, gate_ref, wg_ref, wu_ref, wd_ref,
                o_ref, h_scr):
    bt, tl, D = x_ref.shape
    f = pl.program_id(2)
    last = pl.num_programs(2) - 1

    def partial_ffn(h2):
        gg = jnp.dot(h2, wg_ref[...], preferred_element_type=F32)
        uu = jnp.dot(h2, wu_ref[...], preferred_element_type=F32)
        act = (gg * jax.nn.sigmoid(gg) * uu).astype(BF16)
        return jnp.dot(act, wd_ref[...], preferred_element_type=F32).reshape(bt, -1, D)

    @pl.when(f == 0)
    def _():
        for tsl, fr in _row_groups(bt, tl):
            h = _norm_mod(x_ref[:, tsl, :], g2_ref[...], sc_ref[...], sh_ref[...])
            h2 = h.reshape(-1, D).astype(BF16)
            h_scr[fr, :] = h2
            o_ref[:, tsl, :] = partial_ffn(h2)

    @pl.when(jnp.logical_and(f > 0, f < last))
    def _():
        o_ref[...] += partial_ffn(h_scr[...])

    @pl.when(f == last)
    def _():
        for tsl, fr in _row_groups(bt, tl):
            y = o_ref[:, tsl, :] + partial_ffn(h_scr[fr, :])
            o_ref[:, tsl, :] = x_ref[:, tsl, :] + gate_ref[...] * _rms_gain(y, g3_ref[...])


def _ffn(x, g2, sc, sh, g3, gate, wg, wu, wd, layer):
    B, L, D = x.shape
    F = wg.shape[-1]
    bt, tl = _row_blocking(B, L, FFN_ROW_TILE)
    rows = bt * tl
    tf = FFN_COL_TILE
    assert F % tf == 0 and F // tf >= 2
    row_map = lambda b, l, f: (b, l, 0)
    mod_map = lambda b, l, f: (b, 0, 0)
    vec = lambda: _resident((1, D), lambda b, l, f: (0, 0))
    vmem = 4 * rows * D * 4 + 3 * 2 * D * tf * 2 + rows * D * 2 + 3 * rows * tf * 4
    return pl.pallas_call(
        _ffn_kernel,
        grid=(B // bt, L // tl, F // tf),
        in_specs=[pl.BlockSpec((bt, tl, D), row_map), vec(),
                  pl.BlockSpec((bt, 1, D), mod_map), pl.BlockSpec((bt, 1, D), mod_map),
                  vec(), pl.BlockSpec((bt, 1, D), mod_map),
                  pl.BlockSpec((None, D, tf), lambda b, l, f: (layer, 0, f)),
                  pl.BlockSpec((None, D, tf), lambda b, l, f: (layer, 0, f)),
                  pl.BlockSpec((None, tf, D), lambda b, l, f: (layer, f, 0))],
        out_specs=pl.BlockSpec((bt, tl, D), row_map),
        out_shape=jax.ShapeDtypeStruct((B, L, D), F32),
        scratch_shapes=[pltpu.VMEM((rows, D), BF16)],
        compiler_params=_params(vmem, 3),
        name="swiglu_ffn",
    )(x, g2, sc, sh, g3, gate, wg, wu, wd)


def _log_sigmoid(x):
    return -(jnp.maximum(-x, 0.0) + jnp.log1p(jnp.exp(-jnp.abs(x))))


def _cumsum_rows(x):
    n = x.shape[0]
    row = lax.broadcasted_iota(jnp.int32, x.shape, 0)
    s = 1
    while s < n:
        x = x + jnp.where(row >= s, pltpu.roll(x, s, 0), 0.0)
        s *= 2
    return x


def _mlstm_kernel(q_ref, k_ref, v_ref, o_ref, gt_ref, ghn_ref, *rest, valid_len, has_state):
    if has_state:
        c0_ref, n0_ref, m0_ref, *rest = rest
    x_ref, g1_ref, gate_ref, wout_ref, xo_ref, c_ref, n_ref, m_ref, a_scr = rest
    H, DK, DV = c_ref.shape
    Lc = q_ref.shape[0]
    scale = DK ** -0.5
    c = pl.program_id(1)
    scanning = c < pl.num_programs(1) - 1

    @pl.when(c == 0)
    def _():
        if has_state:
            c_ref[...] = c0_ref[...]
            n_ref[...] = n0_ref[...]
            m_ref[...] = m0_ref[...]
        else:
            c_ref[...] = jnp.zeros_like(c_ref)
            n_ref[...] = jnp.zeros_like(n_ref)
            m_ref[...] = jnp.zeros_like(m_ref)
        a_scr[...] = jnp.zeros_like(a_scr)

    gl = gt_ref[...]
    li_all = gl
    lf_all = _log_sigmoid(gl)
    if valid_len < Lc:
        valid = lax.broadcasted_iota(jnp.int32, gl.shape, 0) < valid_len
        li_all = jnp.where(valid, li_all, -jnp.inf)
        lf_all = jnp.where(valid, lf_all, 0.0)
    b_all = _cumsum_rows(lf_all)
    causal = (lax.broadcasted_iota(jnp.int32, (Lc, Lc), 0)
              >= lax.broadcasted_iota(jnp.int32, (Lc, Lc), 1))

    heads = range(H)
    q = [q_ref[:, h * DK:(h + 1) * DK] for h in heads]
    k = [k_ref[:, h * DK:(h + 1) * DK] for h in heads]
    v = [v_ref[:, h * DV:(h + 1) * DV] for h in heads]
    C = [c_ref[h] for h in heads]
    n = [n_ref[h] for h in heads]
    m = [m_ref[h] for h in heads]

    qk, qC, y = [], [], None
    for h in heads:
        qk.append(lax.dot_general(q[h], k[h], (((1,), (1,)), ((), ())),
                                  preferred_element_type=F32))
        qC.append(jnp.dot(q[h], C[h].astype(BF16), preferred_element_type=F32))
        yh = jnp.dot(a_scr[:, h * DV:(h + 1) * DV], wout_ref[h * DV:(h + 1) * DV, :],
                     preferred_element_type=F32)
        y = yh if y is None else y + yh
    xo_ref[...] = x_ref[...] + gate_ref[...] * _rms_gain(y, g1_ref[...])

    b, g, inter, m_t, w = [], [], [], [], []
    for h in heads:
        b.append(b_all[:, H + h:H + h + 1])
        g.append(li_all[:, h:h + 1] - b[h])
        g_row = jnp.transpose(jnp.broadcast_to(g[h], (Lc, LANES)))[0:1, :]
        dmat = jnp.where(causal, b[h] + g_row, -jnp.inf)
        inter.append(b[h] + m[h])
        m_t.append(jnp.maximum(inter[h], jnp.max(dmat, axis=-1, keepdims=True)))
        w.append(jnp.exp(dmat - m_t[h]))

    for h in heads:
        s = qk[h] * scale * w[h]
        den_s = jnp.sum(s, axis=-1, keepdims=True)
        sv = jnp.dot(s.astype(BF16), v[h], preferred_element_type=F32)
        m_new = m_t[h][Lc - 1:Lc, :]
        w_last = jnp.exp(b[h][Lc - 1:Lc, :] + g[h] - m_new)
        decay = jnp.exp(inter[h][Lc - 1:Lc, :] - m_new)
        kw = k[h].astype(F32) * w_last
        c_new = decay * C[h] + lax.dot_general(
            kw.astype(BF16), v[h], (((0,), (0,)), ((), ())), preferred_element_type=F32)
        c_ref[h] = jnp.where(scanning, c_new, C[h])
        n_ref[h] = jnp.where(scanning, decay * n[h] + jnp.sum(kw, axis=0, keepdims=True), n[h])
        m_ref[h] = jnp.where(scanning, m_new, m[h])

        inter_w = jnp.exp(inter[h] - m_t[h]) * scale
        num = sv + inter_w * qC[h]
        den = den_s + inter_w * jnp.sum(q[h].astype(F32) * n[h], axis=-1, keepdims=True)
        floor = jnp.maximum(jnp.abs(den), jnp.exp(-m_t[h]))
        hh = num * (1.0 / floor)
        hn = _rms_gain(hh, ghn_ref[:, h * DV:(h + 1) * DV])
        og = o_ref[:, h * DV:(h + 1) * DV].astype(F32)
        a_scr[:, h * DV:(h + 1) * DV] = (hn * jax.nn.sigmoid(og)).astype(BF16)


def _mlstm_mixer(z, gates, ghn, dims, state, layer, x, g1, gate, wout):
    B, L, D = x.shape
    H, DK, DV = dims
    HK, HV = H * DK, H * DV
    assert HV == 2 * HK and z.shape[-1] == 2 * HK + 2 * HV
    valid_len = L
    if L >= MLSTM_CHUNK:
        Lc = MLSTM_CHUNK
        assert L % Lc == 0
    else:
        Lc = LANES
        pad = ((0, 0), (0, Lc - L), (0, 0))
        z, gates, x = jnp.pad(z, pad), jnp.pad(gates, pad), jnp.pad(x, pad)
    nc = z.shape[1] // Lc
    blk = lambda width, idx: pl.BlockSpec(
        (None, Lc, width), lambda b, c: (b, jnp.minimum(c, nc - 1), idx))
    fin = pl.BlockSpec((None, Lc, D), lambda b, c: (b, jnp.maximum(c - 1, 0), 0))
    st3 = lambda d1, d2: pl.BlockSpec((None, H, d1, d2), lambda b, c: (b, 0, 0, 0))
    vmem = (2 * Lc * (2 * HK + 2 * HV) * 2 + 2 * Lc * LANES * 4 + 4 * Lc * D * 4
            + 4 * H * DK * DV * 4 + 12 * Lc * Lc * 4 + 8 * Lc * DV * 4 + 2 * DK * DV * 4
            + HV * D * 2 + 3 * Lc * D * 4)
    state_specs, state_args = [], []
    if state is not None:
        C0, n0, m0 = state
        state_specs = [pl.BlockSpec((None, None, H, DK, DV), lambda b, c: (layer, b, 0, 0, 0)),
                       st3(1, DK), st3(1, 1)]
        state_args = [C0, n0.reshape(B, H, 1, DK), m0.reshape(B, H, 1, 1)]
    xo, C, n, m = pl.pallas_call(
        functools.partial(_mlstm_kernel, valid_len=valid_len, has_state=state is not None),
        grid=(B, nc + 1),
        in_specs=[blk(HK, 0), blk(HK, 1), blk(HV, 1), blk(HV, 2), blk(LANES, 0),
                  _resident((1, HV), lambda b, c: (0, 0)), *state_specs,
                  fin, _resident((1, D), lambda b, c: (0, 0)),
                  pl.BlockSpec((None, 1, D), lambda b, c: (b, 0, 0)),
                  _resident((None, HV, D), lambda b, c: (layer, 0, 0))],
        out_specs=[fin, st3(DK, DV), st3(1, DK), st3(1, 1)],
        out_shape=[jax.ShapeDtypeStruct((B, nc * Lc, D), F32),
                   jax.ShapeDtypeStruct((B, H, DK, DV), F32),
                   jax.ShapeDtypeStruct((B, H, 1, DK), F32),
                   jax.ShapeDtypeStruct((B, H, 1, 1), F32)],
        scratch_shapes=[pltpu.VMEM((Lc, HV), BF16)],
        compiler_params=_params(vmem, 2),
        name="mlstm_mixer",
    )(z, z, z, z, gates, ghn, *state_args, x, g1, gate, wout)
    return xo[:, :L], C, n.reshape(B, H, DK), m.reshape(B, H)


def _conv_kernel(gb_ref, gc_ref, u_ref, w_ref, prev_ref, x_ref, g1_ref, gate_ref, wout_ref,
                 xo_ref, st_ref):
    tl, D = gc_ref.shape
    W = w_ref.shape[0]

    @pl.when(pl.program_id(1) == 0)
    def _():
        st_ref[...] = prev_ref[...]

    z = gc_ref[...].astype(F32) * u_ref[...].astype(F32)
    row = lax.broadcasted_iota(jnp.int32, (tl, D), 0)
    conv = z * w_ref[W - 1:W, :]
    for d in range(1, W):
        zd = pltpu.roll(z, d, 0)
        for r in range(d):
            zd = jnp.where(row == r, st_ref[W - 1 - d + r:W - d + r, :], zd)
        conv = conv + zd * w_ref[W - 1 - d:W - d, :]
    a = (gb_ref[...].astype(F32) * conv).astype(BF16)
    st_ref[...] = z[tl - (W - 1):, :]
    y = jnp.dot(a, wout_ref[...], preferred_element_type=F32)
    xo_ref[...] = x_ref[...] + gate_ref[...] * _rms_gain(y, g1_ref[...])


def _conv_mixer(z3, w_conv, prev, x, g1, gate, wout, layer):
    B, L, D = x.shape
    W = w_conv.shape[1]
    tl = min(L, ROW_TILE)
    assert L % tl == 0 and tl >= W - 1 and z3.shape[-1] == 3 * D
    blk = lambda idx: pl.BlockSpec((None, tl, D), lambda b, l: (b, l, idx))
    st = pl.BlockSpec((None, W - 1, D), lambda b, l: (b, 0, 0))
    vmem = 2 * 3 * tl * D * 2 + 4 * tl * D * 4 + D * D * 2 + 6 * tl * D * 4
    return pl.pallas_call(
        _conv_kernel,
        grid=(B, L // tl),
        in_specs=[blk(0), blk(1), blk(2), _resident((W, D), lambda b, l: (0, 0)), st,
                  blk(0), _resident((1, D), lambda b, l: (0, 0)),
                  pl.BlockSpec((None, 1, D), lambda b, l: (b, 0, 0)),
                  _resident((None, D, D), lambda b, l: (layer, 0, 0))],
        out_specs=[blk(0), st],
        out_shape=[jax.ShapeDtypeStruct((B, L, D), F32),
                   jax.ShapeDtypeStruct((B, W - 1, D), prev.dtype)],
        compiler_params=_params(vmem, 2),
        name="conv_mixer",
    )(z3, z3, z3, jnp.transpose(w_conv), prev, x, g1, gate, wout)


def _s5_disc_kernel(ar_ref, ai_ref, ldt_ref, br_ref, bi_ref, cr_ref, ci_ref, tn_ref, tp_ref,
                    abr_ref, abi_ref, bm_ref, cm_ref):
    G, P, N = br_ref.shape
    KT, KW, SW2 = bm_ref.shape
    GP, SW = KW // P, SW2 // 2
    dt = jnp.exp(ldt_ref[...])
    lr, lim = ar_ref[...], ai_ref[...]
    mag = jnp.exp(lr * dt)
    ab_re, ab_im = mag * jnp.cos(lim * dt), mag * jnp.sin(lim * dt)
    den = lr * lr + lim * lim
    nr = ab_re - 1.0
    fr = (nr * lr + ab_im * lim) / den
    fi = (ab_im * lr - nr * lim) / den
    abr_ref[...] = ab_re
    abi_ref[...] = ab_im
    br, bi = br_ref[...], bi_ref[...]
    bb = (fr[:, None, :] * br - fi[:, None, :] * bi, fr[:, None, :] * bi + fi[:, None, :] * br)
    cc = (cr_ref[...], -ci_ref[...])

    b_same = (lax.broadcasted_iota(jnp.int32, (KW, SW), 0) // P
              == lax.broadcasted_iota(jnp.int32, (KW, SW), 1) // N)
    c_same = (lax.broadcasted_iota(jnp.int32, (SW, KW), 0) // N
              == lax.broadcasted_iota(jnp.int32, (SW, KW), 1) // P)
    for kt in range(KT):
        for ri in range(2):
            bk = bb[ri][kt * GP:(kt + 1) * GP].reshape(KW, N).astype(BF16)
            spread = jnp.dot(bk, tn_ref[...], preferred_element_type=F32)
            bm_ref[kt, :, ri * SW:(ri + 1) * SW] = jnp.where(b_same, spread, 0.0).astype(BF16)
            ck = cc[ri][kt * GP:(kt + 1) * GP].reshape(SW, P).astype(BF16)
            spread = jnp.dot(ck, tp_ref[...], preferred_element_type=F32)
            cm_ref[kt, ri * SW:(ri + 1) * SW, :] = jnp.where(c_same, spread, 0.0).astype(BF16)


def _s5_kernel(x_ref, g_ref, sc_ref, sh_ref, bm_ref, cm_ref, ar_ref, ai_ref, dsk_ref,
               s0r_ref, s0i_ref, g1_ref, gate_ref, wa_ref, wb_ref,
               xo_ref, sr_ref, si_ref, h_scr, bu_scr, y_scr, xt_scr):
    B, T, D = x_ref.shape
    KT, KW, SW2 = bm_ref.shape
    SW = SW2 // 2
    rows = T * B
    LG = S5_LANE_GROUP * LANES
    c = pl.program_id(0)
    scanning = c < pl.num_programs(0) - 1

    @pl.when(c == 0)
    def _():
        sr_ref[...] = s0r_ref[...]
        si_ref[...] = s0i_ref[...]
        y_scr[...] = jnp.zeros_like(y_scr)
        xt_scr[...] = jnp.zeros_like(xt_scr)

    yg_prev = y_scr[...]
    xt_prev = xt_scr[...]

    xt = pltpu.einshape("btd->tbd", x_ref[...])
    xt_scr[...] = xt
    h_scr[...] = _norm_mod(xt, g_ref[...], sc_ref[...], sh_ref[...]).reshape(rows, D)
    z_cols = []

    for kt in range(KT):
        cols = slice(kt * KW, (kt + 1) * KW)
        bu = bu_scr.at[kt % 2]
        hk = h_scr[:, cols]
        bu[...] = jnp.dot(hk.astype(BF16), bm_ref[kt], preferred_element_type=F32)
        z_cols.append(jnp.dot(yg_prev, wa_ref[:, cols], preferred_element_type=F32)
                      * jax.nn.sigmoid(jnp.dot(yg_prev, wb_ref[:, cols],
                                               preferred_element_type=F32)))
        for lg in range(SW // LG):
            re_cols = slice(lg * LG, (lg + 1) * LG)
            im_cols = slice(SW + lg * LG, SW + (lg + 1) * LG)
            a_r = jnp.broadcast_to(ar_ref[kt, :, re_cols], (B, LG))
            a_i = jnp.broadcast_to(ai_ref[kt, :, re_cols], (B, LG))
            xr0, xi0 = sr_ref[kt, :, re_cols], si_ref[kt, :, re_cols]
            xr, xi = xr0, xi0
            for t in range(T):
                r = slice(t * B, (t + 1) * B)
                xr, xi = (a_r * xr - a_i * xi + bu[r, re_cols],
                          a_r * xi + a_i * xr + bu[r, im_cols])
                bu[r, re_cols] = xr
                bu[r, im_cols] = xi
            sr_ref[kt, :, re_cols] = jnp.where(scanning, xr, xr0)
            si_ref[kt, :, re_cols] = jnp.where(scanning, xi, xi0)
        yk = jnp.dot(bu[...].astype(BF16), cm_ref[kt], preferred_element_type=F32)
        yk = yk + dsk_ref[:, cols] * hk
        y_scr[:, cols] = jax.nn.gelu(yk).astype(y_scr.dtype)

    z = jnp.concatenate(z_cols, axis=1)
    out = xt_prev + gate_ref[...] * _rms_gain(z, g1_ref[...]).reshape(T, B, D)
    xo_ref[...] = pltpu.einshape("tbd->btd", out)


def _s5_mixer(x, g, sc, sh, bmat, cmat, a_r, a_i, dsk, s0r, s0i, g1, gate, wout, layer):
    B, L, D = x.shape
    KT, KW, SW2 = bmat.shape
    SW = SW2 // 2
    assert B == SUBLANES and SW % (S5_LANE_GROUP * LANES) == 0
    T = min(S5_CHUNK, L)
    assert L % T == 0
    rows = T * B
    c0 = lambda *shape: _resident(shape, lambda c: (0,) * len(shape))
    vmem = (4 * rows * D * 4 + 2 * KT * KW * SW2 * 2 + 2 * D * D * 2 + rows * D * 4
            + 2 * rows * SW2 * 4 + rows * D * 2 + 5 * rows * D * 4 + 6 * KT * B * SW * 4)
    nc = L // T
    return pl.pallas_call(
        _s5_kernel,
        grid=(nc + 1,),
        in_specs=[pl.BlockSpec((B, T, D), lambda c: (0, jnp.minimum(c, nc - 1), 0)),
                  c0(1, D), c0(B, D), c0(B, D),
                  c0(KT, KW, SW2), c0(KT, SW2, KW), c0(KT, 1, SW), c0(KT, 1, SW), c0(1, D),
                  c0(KT, B, SW), c0(KT, B, SW), c0(1, D), c0(B, D),
                  _resident((None, D, D), lambda c: (layer, 0, 0)),
                  _resident((None, D, D), lambda c: (layer, 0, 1))],
        out_specs=[pl.BlockSpec((B, T, D), lambda c: (0, jnp.maximum(c - 1, 0), 0)),
                   pl.BlockSpec((KT, B, SW), lambda c: (0, 0, 0)),
                   pl.BlockSpec((KT, B, SW), lambda c: (0, 0, 0))],
        out_shape=[jax.ShapeDtypeStruct((B, L, D), F32),
                   jax.ShapeDtypeStruct((KT, B, SW), F32),
                   jax.ShapeDtypeStruct((KT, B, SW), F32)],
        scratch_shapes=[pltpu.VMEM((rows, D), F32), pltpu.VMEM((2, rows, SW2), F32),
                        pltpu.VMEM((rows, D), BF16), pltpu.VMEM((T, B, D), F32)],
        compiler_params=_params(vmem, 1),
        name="s5_mixer",
    )(x, g, sc, sh, bmat, cmat, a_r, a_i, dsk, s0r, s0i, g1, gate, wout, wout)


def _s5_weights(a_re, a_im, b_re, b_im, c_re, c_im, d_skip, log_dt):
    G, N, P = b_re.shape
    KW = MXU_DIM_V7X
    GP = KW // P
    KT, SW = G // GP, GP * N
    full = lambda *shape: pl.BlockSpec(shape, lambda: (0,) * len(shape))
    tile_n = jnp.tile(jnp.eye(N, dtype=BF16), (1, GP))
    tile_p = jnp.tile(jnp.eye(P, dtype=BF16), (1, GP))
    ab_re, ab_im, bmat, cmat = pl.pallas_call(
        _s5_disc_kernel,
        in_specs=[full(G, N), full(G, N), full(G, 1), full(G, P, N), full(G, P, N),
                  full(G, N, P), full(G, N, P), full(N, SW), full(P, KW)],
        out_specs=[full(G, N), full(G, N), full(KT, KW, 2 * SW), full(KT, 2 * SW, KW)],
        out_shape=[jax.ShapeDtypeStruct((G, N), F32), jax.ShapeDtypeStruct((G, N), F32),
                   jax.ShapeDtypeStruct((KT, KW, 2 * SW), BF16),
                   jax.ShapeDtypeStruct((KT, 2 * SW, KW), BF16)],
        compiler_params=pltpu.CompilerParams(vmem_limit_bytes=VMEM_CAP),
        name="s5_discretize",
    )(a_re, a_im, log_dt.reshape(G, 1), jnp.swapaxes(b_re, 1, 2), jnp.swapaxes(b_im, 1, 2),
      jnp.swapaxes(c_re, 1, 2), jnp.swapaxes(c_im, 1, 2), tile_n, tile_p)
    return (bmat, cmat, ab_re.reshape(KT, 1, SW), ab_im.reshape(KT, 1, SW),
            d_skip.reshape(1, G * P))


def _trunk(x, mod, row0, st_C, st_n, st_m, st_conv, st_re, st_im, p):
    B, L, D = x.shape
    depth = p['g_norm'].shape[0]
    new_C, new_n, new_m, new_conv, new_re, new_im = [], [], [], [], [], []
    for i in range(depth):
        m6 = mod[i, row0:row0 + B].reshape(B, 6, 1, D)
        sh1, sc1, g1, sh2, sc2, g2 = (m6[:, j] for j in range(6))
        gn = p['g_norm'][i]
        gvec = lambda r: gn[r].reshape(1, D)
        kind, j = i % N_MIXERS, i // N_MIXERS
        if kind == 0:
            H = p['bA_gates'].shape[-1] // 2
            HV = p['wA_out'].shape[1]
            nz = p['wA_in'].shape[-1] - 2 * H
            dims = (H, (nz - 2 * HV) // (2 * H), HV // H)
            state = None if st_C is None else (st_C, st_n[j], st_m[j])
            w_gates = jnp.pad(p['wA_in'][j, :, nz:], ((0, 0), (0, LANES - 2 * H)))
            b_gates = jnp.pad(p['bA_gates'][j], (0, LANES - 2 * H)).reshape(1, LANES)
            z, gates = _inproj(x, gvec(0), sc1, sh1, p['wA_in'], j, nz, w_gates, b_gates)
            x, C, n, m = _mlstm_mixer(z, gates, p['gA_hnorm'][j].reshape(1, -1), dims, state, j,
                                      x, gvec(1), g1, p['wA_out'])
            new_C.append(C); new_n.append(n); new_m.append(m)
        elif kind == 1:
            z3 = _inproj(x, gvec(0), sc1, sh1, p['wB_in'], j, p['wB_in'].shape[-1])
            x, cv = _conv_mixer(z3, p['wB_conv'][j], st_conv[j], x, gvec(1), g1,
                                p['wB_out'], j)
            new_conv.append(cv)
        else:
            bmat, cmat, a_r, a_i, dsk = _s5_weights(
                p['s5_A_re'][j], p['s5_A_im'][j], p['s5_B_re'][j], p['s5_B_im'][j],
                p['s5_C_re'][j], p['s5_C_im'][j], p['s5_D'][j], p['s5_log_dt'][j])
            KT, _, SW2 = bmat.shape
            to_lanes = lambda s: jnp.swapaxes(s.reshape(B, KT, SW2 // 2), 0, 1)
            x, sr, si = _s5_mixer(x, gvec(0), sc1.reshape(B, D), sh1.reshape(B, D), bmat, cmat,
                                  a_r, a_i, dsk, to_lanes(st_re[j]), to_lanes(st_im[j]),
                                  gvec(1), g1.reshape(B, D), p['wC_out'], j)
            from_lanes = lambda s: jnp.swapaxes(s, 0, 1).reshape(st_re[j].shape)
            new_re.append(from_lanes(sr)); new_im.append(from_lanes(si))
        x = _ffn(x, gvec(2), sc2, sh2, gvec(3), g2, p['w_ffn_gate'], p['w_ffn_up'],
                 p['w_ffn_down'], i)
    return (x, jnp.stack(new_C), jnp.stack(new_n), jnp.stack(new_m), jnp.stack(new_conv),
            jnp.stack(new_re), jnp.stack(new_im))


def kernel(x_prompt, x_sample, state_mlstm_C, state_mlstm_n, state_mlstm_m, state_conv,
           state_s5_re, state_s5_im, c_prompt, c_sample, w_mod, b_mod, g_norm, wA_in,
           bA_gates, gA_hnorm, wA_out, wB_in, wB_conv, wB_out, s5_A_re, s5_A_im, s5_B_re,
           s5_B_im, s5_C_re, s5_C_im, s5_D, s5_log_dt, wC_out, w_ffn_gate, w_ffn_up,
           w_ffn_down):
    cast = lambda w: w.astype(BF16)
    p = dict(g_norm=g_norm, wA_in=cast(wA_in), bA_gates=bA_gates, gA_hnorm=gA_hnorm,
             wA_out=cast(wA_out), wB_in=cast(wB_in), wB_conv=wB_conv, wB_out=cast(wB_out),
             s5_A_re=s5_A_re, s5_A_im=s5_A_im, s5_B_re=s5_B_re, s5_B_im=s5_B_im,
             s5_C_re=s5_C_re, s5_C_im=s5_C_im, s5_D=s5_D, s5_log_dt=s5_log_dt,
             wC_out=cast(wC_out), w_ffn_gate=cast(w_ffn_gate), w_ffn_up=cast(w_ffn_up),
             w_ffn_down=cast(w_ffn_down))
    bp = x_prompt.shape[0]
    mod = _modulation(jnp.concatenate([c_prompt, c_sample], axis=0), w_mod, b_mod)
    zeros = lambda s: jnp.zeros((s.shape[0], bp) + s.shape[2:], s.dtype)
    outs_p = _trunk(x_prompt, mod, 0, None, None, None, zeros(state_conv),
                    zeros(state_s5_re), zeros(state_s5_im), p)
    outs_s = _trunk(x_sample, mod, bp, state_mlstm_C, state_mlstm_n, state_mlstm_m,
                    state_conv, state_s5_re, state_s5_im, p)
    return (outs_p[0], outs_s[0]) + tuple(outs_p[1:]) + tuple(outs_s[1:])
```
